```python
import math
import jax, jax.numpy as jnp
from jax import lax
import numpy as np

D_MODEL = 2048
BATCH = 8
SEQ = 2048
DEPTH = 4

N_MIXERS = 2
D_FF = 5632
ATT_HEADS = 32
ATT_KV_HEADS = 4
ATT_HEAD_DIM = D_MODEL // ATT_HEADS
ATT_GROUP = ATT_HEADS // ATT_KV_HEADS
WINDOW = 128
BLOCK = WINDOW
ATT_QKV_COLS = ATT_HEADS * ATT_HEAD_DIM + 2 * ATT_KV_HEADS * ATT_HEAD_DIM
M_HEADS = 4
M_V_DIM = D_MODEL // M_HEADS
M_QK_DIM = M_V_DIM // 2
CHUNK = 128
M_PROJ_COLS = 2 * M_HEADS * M_QK_DIM + 2 * M_HEADS * M_V_DIM + 2 * M_HEADS
N_ATT_LAYERS = (DEPTH + 1) // 2
N_MLSTM_LAYERS = DEPTH // 2
DEEPNORM_ALPHA = (2.0 * DEPTH) ** 0.25
DEEPNORM_BETA = (8.0 * DEPTH) ** -0.25
LN_EPS = 1e-5

kernel_name = "hybrid_swa_sink_alibi_mlstm_macaron_deepnorm"


def layer_norm(x, g, b):
    xf = x.astype(jnp.float32)
    mu = jnp.mean(xf, axis=-1, keepdims=True)
    var = jnp.mean(jnp.square(xf - mu), axis=-1, keepdims=True)
    return ((xf - mu) * lax.rsqrt(var + LN_EPS) * g.astype(jnp.float32) + b.astype(jnp.float32)).astype(x.dtype)


def swiglu(x, w1, w3, w2):
    return (jax.nn.silu(x @ w1) * (x @ w3)) @ w2


def alibi_slopes(n_heads):
    return 2.0 ** (-8.0 * jnp.arange(1, n_heads + 1, dtype=jnp.float32) / n_heads)


def sliding_window_attention(x, w_qkv, sinks, w_o):
    B, S, _ = x.shape
    H, KV, G, Dh = ATT_HEADS, ATT_KV_HEADS, ATT_GROUP, ATT_HEAD_DIM
    nb = S // BLOCK
    qkv = x @ w_qkv
    q, k, v = jnp.split(qkv, [H * Dh, H * Dh + KV * Dh], axis=-1)
    q = q.reshape(B, nb, BLOCK, KV, G, Dh)
    k = k.reshape(B, nb, BLOCK, KV, Dh)
    v = v.reshape(B, nb, BLOCK, KV, Dh)
    pad = jnp.zeros_like(k[:, :1])
    kb = jnp.concatenate([jnp.concatenate([pad, k[:, :-1]], axis=1), k], axis=2)
    vb = jnp.concatenate([jnp.concatenate([pad, v[:, :-1]], axis=1), v], axis=2)
    s = jnp.einsum('bnqkgd,bnskd->bnkgqs', q, kb).astype(jnp.float32) * (Dh ** -0.5)
    qi = jnp.arange(BLOCK)[:, None]
    kj = jnp.arange(2 * BLOCK)[None, :]
    dist = BLOCK + qi - kj
    in_window = (dist >= 0) & (dist < WINDOW)
    has_prev = (jnp.arange(nb)[:, None, None] > 0) | (kj >= BLOCK)[None]
    valid = in_window[None] & has_prev
    slopes = alibi_slopes(H).reshape(KV, G)
    s = s - slopes[:, :, None, None] * dist.astype(jnp.float32)
    s = jnp.where(valid[None, :, None, None], s, -jnp.inf)
    sink = sinks.astype(jnp.float32).reshape(KV, G)[:, :, None, None]
    m = jnp.maximum(jnp.max(s, axis=-1, keepdims=True), sink)
    p = jnp.exp(s - m)
    p = p / (jnp.sum(p, axis=-1, keepdims=True) + jnp.exp(sink - m))
    o = jnp.einsum('bnkgqs,bnskd->bnqkgd', p.astype(x.dtype), vb).reshape(B, S, H * Dh)
    return o @ w_o


def mlstm(x, w_in, b_gates, w_o):
    B, S, _ = x.shape
    H, Dk, Dv, L = M_HEADS, M_QK_DIM, M_V_DIM, CHUNK
    nc = S // L
    f32 = jnp.float32
    proj = x @ w_in
    q, k, v, og, gates = jnp.split(
        proj, [H * Dk, 2 * H * Dk, 2 * H * Dk + H * Dv, 2 * H * Dk + 2 * H * Dv], axis=-1)
    gates = gates.astype(f32) + b_gates.astype(f32)
    i_pre = gates[..., :H]
    log_f = jax.nn.log_sigmoid(gates[..., H:])

    def to_chunks(t, d):
        return t.astype(f32).reshape(B, nc, L, H, d).transpose(1, 0, 3, 2, 4)

    qc = to_chunks(q, Dk)
    kc = to_chunks(k, Dk) * (Dk ** -0.5)
    vc = to_chunks(v, Dv)
    ic = i_pre.reshape(B, nc, L, H).transpose(1, 0, 3, 2)
    fc = log_f.reshape(B, nc, L, H).transpose(1, 0, 3, 2)
    causal = jnp.tril(jnp.ones((L, L), dtype=bool))

    def step(carry, inp):
        C, n, m = carry
        qt, kt, vt, ig, lf = inp
        b = jnp.cumsum(lf, axis=-1)
        a = b + m[..., None]
        dmat = b[..., :, None] - b[..., None, :] + ig[..., None, :]
        dmat = jnp.where(causal, dmat, -jnp.inf)
        m_rows = jnp.maximum(a, jnp.max(dmat, axis=-1))
        inter = jnp.exp(a - m_rows)
        w = jnp.einsum('bhtd,bhsd->bhts', qt, kt) * jnp.exp(dmat - m_rows[..., None])
        num = inter[..., None] * jnp.einsum('bhtd,bhde->bhte', qt, C) + jnp.einsum('bhts,bhse->bhte', w, vt)
        den = inter * jnp.einsum('bhtd,bhd->bht', qt, n) + jnp.sum(w, axis=-1)
        h = num / jnp.maximum(jnp.abs(den), jnp.exp(-m_rows))[..., None]
        m_end = m_rows[..., -1]
        decay = jnp.exp(b[..., -1] + m - m_end)
        wk = jnp.exp(b[..., -1:] - b + ig - m_end[..., None])
        C = decay[..., None, None] * C + jnp.einsum('bhs,bhsd,bhse->bhde', wk, kt, vt)
        n = decay[..., None] * n + jnp.einsum('bhs,bhsd->bhd', wk, kt)
        return (C, n, m_end), h

    init = (jnp.zeros((B, H, Dk, Dv), f32), jnp.zeros((B, H, Dk), f32), jnp.zeros((B, H), f32))
    _, hs = lax.scan(step, init, (qc, kc, vc, ic, fc))
    h = hs.transpose(1, 0, 3, 2, 4).reshape(B, S, H * Dv).astype(x.dtype)
    return (h * jax.nn.sigmoid(og)) @ w_o


def setup_inputs(seed: int = 0) -> dict:
    key = jax.random.key(seed)
    ks = jax.random.split(key, 16)
    D, F = D_MODEL, D_FF
    beta = DEEPNORM_BETA
    x = jax.random.normal(ks[0], (BATCH, SEQ, D), jnp.float32)
    ffn_w1 = jax.random.normal(ks[1], (DEPTH, 2, D, F), jnp.float32) * (D ** -0.5) * beta
    ffn_w3 = jax.random.normal(ks[2], (DEPTH, 2, D, F), jnp.float32) * (D ** -0.5) * beta
    ffn_w2 = jax.random.normal(ks[3], (DEPTH, 2, F, D), jnp.float32) * (F ** -0.5) * beta
    ln_g = 1.0 + 0.02 * jax.random.normal(ks[4], (DEPTH, 3, D), jnp.float32)
    ln_b = 0.02 * jax.random.normal(ks[5], (DEPTH, 3, D), jnp.float32)
    att_scale = jnp.concatenate([
        jnp.ones((ATT_HEADS * ATT_HEAD_DIM + ATT_KV_HEADS * ATT_HEAD_DIM,), jnp.float32),
        jnp.full((ATT_KV_HEADS * ATT_HEAD_DIM,), beta, jnp.float32)])
    att_w_qkv = jax.random.normal(ks[6], (N_ATT_LAYERS, D, ATT_QKV_COLS), jnp.float32) * (D ** -0.5) * att_scale
    att_sinks = 0.5 * jax.random.normal(ks[7], (N_ATT_LAYERS, ATT_HEADS), jnp.float32)
    att_w_o = jax.random.normal(ks[8], (N_ATT_LAYERS, ATT_HEADS * ATT_HEAD_DIM, D), jnp.float32) * (D ** -0.5) * beta
    m_scale = jnp.concatenate([
        jnp.ones((2 * M_HEADS * M_QK_DIM,), jnp.float32),
        jnp.full((M_HEADS * M_V_DIM,), beta, jnp.float32),
        jnp.ones((M_HEADS * M_V_DIM,), jnp.float32),
        jnp.full((2 * M_HEADS,), 0.1, jnp.float32)])
    mlstm_w_in = jax.random.normal(ks[9], (N_MLSTM_LAYERS, D, M_PROJ_COLS), jnp.float32) * (D ** -0.5) * m_scale
    i_bias = 0.1 * jax.random.normal(ks[10], (N_MLSTM_LAYERS, M_HEADS), jnp.float32)
    f_bias = jnp.linspace(3.0, 6.0, M_HEADS, dtype=jnp.float32)[None] + 0.1 * jax.random.normal(ks[11], (N_MLSTM_LAYERS, M_HEADS), jnp.float32)
    mlstm_b_gates = jnp.concatenate([i_bias, f_bias], axis=-1)
    mlstm_w_o = jax.random.normal(ks[12], (N_MLSTM_LAYERS, M_HEADS * M_V_DIM, D), jnp.float32) * (D ** -0.5) * beta
    return {"x": x, "ffn_w1": ffn_w1, "ffn_w3": ffn_w3, "ffn_w2": ffn_w2,
            "ln_g": ln_g, "ln_b": ln_b,
            "att_w_qkv": att_w_qkv, "att_sinks": att_sinks, "att_w_o": att_w_o,
            "mlstm_w_in": mlstm_w_in, "mlstm_b_gates": mlstm_b_gates, "mlstm_w_o": mlstm_w_o}


def reference(x, ffn_w1, ffn_w3, ffn_w2, ln_g, ln_b, att_w_qkv, att_sinks, att_w_o,
              mlstm_w_in, mlstm_b_gates, mlstm_w_o):
    alpha = DEEPNORM_ALPHA
    for layer in range(DEPTH):
        ffn_a = swiglu(x, ffn_w1[layer, 0], ffn_w3[layer, 0], ffn_w2[layer, 0])
        x = layer_norm(alpha * x + 0.5 * ffn_a, ln_g[layer, 0], ln_b[layer, 0])
        j = layer // N_MIXERS
        if layer % N_MIXERS == 0:
            y = sliding_window_attention(x, att_w_qkv[j], att_sinks[j], att_w_o[j])
        else:
            y = mlstm(x, mlstm_w_in[j], mlstm_b_gates[j], mlstm_w_o[j])
        x = layer_norm(alpha * x + y, ln_g[layer, 1], ln_b[layer, 1])
        ffn_b = swiglu(x, ffn_w1[layer, 1], ffn_w3[layer, 1], ffn_w2[layer, 1])
        x = layer_norm(alpha * x + 0.5 * ffn_b, ln_g[layer, 2], ln_b[layer, 2])
    return x
```

```python
import functools

import jax
import jax.numpy as jnp
from jax import lax
from jax.experimental import pallas as pl
from jax.experimental.pallas import tpu as pltpu

D_MODEL = 2048
DEPTH = 4
D_FF = 5632
ATT_HEADS = 32
ATT_KV_HEADS = 4
ATT_GROUP = ATT_HEADS // ATT_KV_HEADS
ATT_HEAD_DIM = D_MODEL // ATT_HEADS
WINDOW = 128
M_HEADS = 4
M_V_DIM = D_MODEL // M_HEADS
M_QK_DIM = M_V_DIM // 2
CHUNK = 128
DEEPNORM_ALPHA = (2.0 * DEPTH) ** 0.25
LN_EPS = 1e-5

LANES = 128
MASKED = -1e30
VMEM_LIMIT_BYTES = 56 * 1024 * 1024

BF16 = jnp.bfloat16
F32 = jnp.float32


def _compiler_params(semantics):
    return pltpu.CompilerParams(dimension_semantics=semantics,
                                vmem_limit_bytes=VMEM_LIMIT_BYTES)


def _layer_norm(y, g, b):
    mu = jnp.mean(y, axis=-1, keepdims=True)
    yc = y - mu
    var = jnp.mean(yc * yc, axis=-1, keepdims=True)
    return yc * lax.rsqrt(var + LN_EPS) * g + b


FFN_TM = 512
FFN_TF = 512


def _ffn_kernel(x_ref, w1_ref, w3_ref, w2_ref, g_ref, b_ref, o_ref, xb_ref):
    f = pl.program_id(1)

    @pl.when(f == 0)
    def _():
        xb_ref[...] = x_ref[...].astype(BF16)
        o_ref[...] = jnp.zeros_like(o_ref)

    xb = xb_ref[...]
    h1 = jnp.dot(xb, w1_ref[...], preferred_element_type=F32)
    h3 = jnp.dot(xb, w3_ref[...], preferred_element_type=F32)
    act = (h1 * jax.nn.sigmoid(h1) * h3).astype(BF16)
    o_ref[...] += jnp.dot(act, w2_ref[...], preferred_element_type=F32)

    @pl.when(f == pl.num_programs(1) - 1)
    def _():
        y = DEEPNORM_ALPHA * x_ref[...] + 0.5 * o_ref[...]
        o_ref[...] = _layer_norm(y, g_ref[...], b_ref[...])


def _ffn_ln(x, w1, w3, w2, g, b, layer, half):
    t = x.shape[0]
    grid = (t // FFN_TM, D_FF // FFN_TF)
    return pl.pallas_call(
        _ffn_kernel,
        grid=grid,
        in_specs=[
            pl.BlockSpec((FFN_TM, D_MODEL), lambda i, f: (i, 0)),
            pl.BlockSpec((None, None, D_MODEL, FFN_TF), lambda i, f: (layer, half, 0, f)),
            pl.BlockSpec((None, None, D_MODEL, FFN_TF), lambda i, f: (layer, half, 0, f)),
            pl.BlockSpec((None, None, FFN_TF, D_MODEL), lambda i, f: (layer, half, f, 0)),
            pl.BlockSpec((1, D_MODEL), lambda i, f: (0, 0)),
            pl.BlockSpec((1, D_MODEL), lambda i, f: (0, 0)),
        ],
        out_specs=pl.BlockSpec((FFN_TM, D_MODEL), lambda i, f: (i, 0)),
        out_shape=jax.ShapeDtypeStruct((t, D_MODEL), F32),
        scratch_shapes=[pltpu.VMEM((FFN_TM, D_MODEL), BF16)],
        compiler_params=_compiler_params(("parallel", "arbitrary")),
        name="ffn_ln",
    )(x, w1, w3, w2, g, b)


OUT_TM = 512


def _outproj_kernel(a_ref, w_ref, x_ref, g_ref, b_ref, o_ref):
    y = jnp.dot(a_ref[...], w_ref[...], preferred_element_type=F32)
    y = DEEPNORM_ALPHA * x_ref[...] + y
    o_ref[...] = _layer_norm(y, g_ref[...], b_ref[...])


def _outproj_ln(a, w, x, g, b):
    t = x.shape[0]
    return pl.pallas_call(
        _outproj_kernel,
        grid=(t // OUT_TM,),
        in_specs=[
            pl.BlockSpec((OUT_TM, D_MODEL), lambda i: (i, 0)),
            pl.BlockSpec((D_MODEL, D_MODEL), lambda i: (0, 0)),
            pl.BlockSpec((OUT_TM, D_MODEL), lambda i: (i, 0)),
            pl.BlockSpec((1, D_MODEL), lambda i: (0, 0)),
            pl.BlockSpec((1, D_MODEL), lambda i: (0, 0)),
        ],
        out_specs=pl.BlockSpec((OUT_TM, D_MODEL), lambda i: (i, 0)),
        out_shape=jax.ShapeDtypeStruct((t, D_MODEL), F32),
        compiler_params=_compiler_params(("parallel",)),
        name="outproj_ln",
    )(a, w, x, g, b)


QKV_TM = 512
ATT_KV_COLS = ATT_KV_HEADS * ATT_HEAD_DIM
ATT_PAIRS = ATT_HEADS // 2


def _qkv_kernel(x_ref, wq_ref, wk_ref, wv_ref, q_ref, k_ref, v_ref):
    xb = x_ref[...].astype(BF16)
    q_ref[...] = jnp.dot(xb, wq_ref[...], preferred_element_type=F32).astype(BF16)
    k_ref[...] = jnp.dot(xb, wk_ref[...], preferred_element_type=F32).astype(BF16)
    v_ref[...] = jnp.dot(xb, wv_ref[...], preferred_element_type=F32).astype(BF16)


def _qkv_proj(x, wq, wk, wv):
    t = x.shape[0]
    return pl.pallas_call(
        _qkv_kernel,
        grid=(t // QKV_TM,),
        in_specs=[
            pl.BlockSpec((QKV_TM, D_MODEL), lambda i: (i, 0)),
            pl.BlockSpec((D_MODEL, D_MODEL), lambda i: (0, 0)),
            pl.BlockSpec((D_MODEL, ATT_KV_COLS), lambda i: (0, 0)),
            pl.BlockSpec((D_MODEL, ATT_KV_COLS), lambda i: (0, 0)),
        ],
        out_specs=[
            pl.BlockSpec((QKV_TM, D_MODEL), lambda i: (i, 0)),
            pl.BlockSpec((QKV_TM, ATT_KV_COLS), lambda i: (i, 0)),
            pl.BlockSpec((QKV_TM, ATT_KV_COLS), lambda i: (i, 0)),
        ],
        out_shape=[
            jax.ShapeDtypeStruct((t, D_MODEL), BF16),
            jax.ShapeDtypeStruct((t, ATT_KV_COLS), BF16),
            jax.ShapeDtypeStruct((t, ATT_KV_COLS), BF16),
        ],
        compiler_params=_compiler_params(("parallel",)),
        name="qkv_proj",
    )(x, wq, wk, wv)


def _alibi_slope(head):
    return 2.0 ** (-8.0 * (head + 1) / ATT_HEADS)


def _attn_kernel(sink_ref, q_ref, kc_ref, kp_ref, vc_ref, vp_ref, o_ref):
    n = pl.program_id(1)
    rows = 2 * WINDOW
    keys = 2 * WINDOW
    r = lax.broadcasted_iota(jnp.int32, (rows, keys), 0)
    kj = lax.broadcasted_iota(jnp.int32, (rows, keys), 1)
    dist = WINDOW + (r & (WINDOW - 1)) - kj
    valid = (dist >= 0) & (dist < WINDOW) & ((kj >= WINDOW) | (n > 0))
    distf = dist.astype(F32)
    is_lo_row = lax.broadcasted_iota(jnp.int32, (rows, 1), 0) < WINDOW
    lo_lane = lax.broadcasted_iota(jnp.int32, (1, LANES), 1) < ATT_HEAD_DIM
    zero = jnp.zeros((), BF16)

    for c2 in range(ATT_KV_HEADS // 2):
        cols = slice(LANES * c2, LANES * (c2 + 1))
        kk = jnp.concatenate([kp_ref[:, cols], kc_ref[:, cols]], axis=0)
        vv = jnp.concatenate([vp_ref[:, cols], vc_ref[:, cols]], axis=0)
        vcat = jnp.concatenate([jnp.where(lo_lane, vv, zero),
                                jnp.where(lo_lane, zero, vv)], axis=0)
        for g in range(ATT_GROUP):
            pair = c2 * ATT_GROUP + g
            head_lo = (2 * c2) * ATT_GROUP + g
            head_hi = (2 * c2 + 1) * ATT_GROUP + g
            pcols = slice(LANES * pair, LANES * (pair + 1))
            qp = q_ref[:, pcols] * jnp.asarray(ATT_HEAD_DIM ** -0.5, BF16)
            qs = jnp.concatenate([jnp.where(lo_lane, qp, zero),
                                  jnp.where(lo_lane, zero, qp)], axis=0)
            s = lax.dot_general(qs, kk, (((1,), (1,)), ((), ())),
                                preferred_element_type=F32)
            slope = jnp.where(is_lo_row, _alibi_slope(head_lo), _alibi_slope(head_hi))
            sink = jnp.where(is_lo_row, sink_ref[head_lo], sink_ref[head_hi])
            s = jnp.where(valid, s - slope * distf, MASKED)
            m = jnp.maximum(jnp.max(s, axis=-1, keepdims=True), sink)
            p = jnp.exp(s - m)
            denom = jnp.sum(p, axis=-1, keepdims=True) + jnp.exp(sink - m)
            p = (p * (1.0 / denom)).astype(BF16)
            pcat = jnp.concatenate([p[:WINDOW], p[WINDOW:]], axis=1)
            o = jnp.dot(pcat, vcat, preferred_element_type=F32)
            o_ref[:, pcols] = o.astype(BF16)


def _attn_core(sinks, q, k, v, batch, seq):
    nb = seq // WINDOW
    cur = lambda b, n: (b * nb + n, 0)
    prev = lambda b, n: (b * nb + jnp.maximum(n - 1, 0), 0)
    return pl.pallas_call(
        _attn_kernel,
        grid=(batch, nb),
        in_specs=[
            pl.BlockSpec(memory_space=pltpu.SMEM),
            pl.BlockSpec((WINDOW, D_MODEL), cur),
            pl.BlockSpec((WINDOW, ATT_KV_COLS), cur),
            pl.BlockSpec((WINDOW, ATT_KV_COLS), prev),
            pl.BlockSpec((WINDOW, ATT_KV_COLS), cur),
            pl.BlockSpec((WINDOW, ATT_KV_COLS), prev),
        ],
        out_specs=pl.BlockSpec((WINDOW, D_MODEL), cur),
        out_shape=jax.ShapeDtypeStruct((batch * seq, D_MODEL), BF16),
        compiler_params=_compiler_params(("parallel", "arbitrary")),
        name="swa_core",
    )(sinks, q, k, k, v, v)


def _head_pair_permutation():
    perm = []
    for c2 in range(ATT_KV_HEADS // 2):
        for g in range(ATT_GROUP):
            for half in range(2):
                head = (2 * c2 + half) * ATT_GROUP + g
                perm.extend(range(head * ATT_HEAD_DIM, (head + 1) * ATT_HEAD_DIM))
    return jnp.asarray(perm, jnp.int32)


def _attention_mixer(x, w_qkv, sinks, w_o, g, b, batch, seq):
    perm = _head_pair_permutation()
    nq = ATT_HEADS * ATT_HEAD_DIM
    wq = w_qkv[:, :nq][:, perm].astype(BF16)
    wk = w_qkv[:, nq:nq + ATT_KV_COLS].astype(BF16)
    wv = w_qkv[:, nq + ATT_KV_COLS:].astype(BF16)
    wo = w_o[perm, :].astype(BF16)
    q, k, v = _qkv_proj(x, wq, wk, wv)
    o = _attn_core(sinks.astype(F32), q, k, v, batch, seq)
    return _outproj_ln(o, wo, x, g, b)


MPROJ_TM = 512
MPROJ_TN = 1024
M_MAIN_COLS = 2 * M_HEADS * M_QK_DIM + 2 * M_HEADS * M_V_DIM
M_GATE_COLS = 2 * M_HEADS


def _mproj_kernel(x_ref, w_ref, wg_ref, bg_ref, p_ref, gates_ref, xb_ref):
    j = pl.program_id(1)

    @pl.when(j == 0)
    def _():
        xb = x_ref[...].astype(BF16)
        xb_ref[...] = xb
        gates_ref[...] = jnp.dot(xb, wg_ref[...], preferred_element_type=F32) + bg_ref[...]

    p_ref[...] = jnp.dot(xb_ref[...], w_ref[...], preferred_element_type=F32).astype(BF16)


def _mlstm_proj(x, w_main, w_gates, b_gates):
    t = x.shape[0]
    return pl.pallas_call(
        _mproj_kernel,
        grid=(t // MPROJ_TM, M_MAIN_COLS // MPROJ_TN),
        in_specs=[
            pl.BlockSpec((MPROJ_TM, D_MODEL), lambda i, j: (i, 0)),
            pl.BlockSpec((D_MODEL, MPROJ_TN), lambda i, j: (0, j)),
            pl.BlockSpec((D_MODEL, LANES), lambda i, j: (0, 0)),
            pl.BlockSpec((1, LANES), lambda i, j: (0, 0)),
        ],
        out_specs=[
            pl.BlockSpec((MPROJ_TM, MPROJ_TN), lambda i, j: (i, j)),
            pl.BlockSpec((MPROJ_TM, LANES), lambda i, j: (i, 0)),
        ],
        out_shape=[
            jax.ShapeDtypeStruct((t, M_MAIN_COLS), BF16),
            jax.ShapeDtypeStruct((t, LANES), F32),
        ],
        scratch_shapes=[pltpu.VMEM((MPROJ_TM, D_MODEL), BF16)],
        compiler_params=_compiler_params(("parallel", "arbitrary")),
        name="mlstm_proj",
    )(x, w_main, w_gates, b_gates)


def _log_sigmoid(x):
    return jnp.minimum(x, 0.0) - jnp.log1p(jnp.exp(-jnp.abs(x)))


def _mlstm_kernel(q_ref, k_ref, v_ref, og_ref, gates_ref, o_ref, c_ref, n_ref, m_ref):
    L = CHUNK

    @pl.when(pl.program_id(1) == 0)
    def _():
        c_ref[...] = jnp.zeros_like(c_ref)
        n_ref[...] = jnp.zeros_like(n_ref)
        m_ref[...] = jnp.zeros_like(m_ref)

    gates = gates_ref[...]
    gates_t = gates.T
    t_idx = lax.broadcasted_iota(jnp.int32, (L, L), 0)
    s_idx = lax.broadcasted_iota(jnp.int32, (L, L), 1)
    causal = s_idx <= t_idx

    for h in range(M_HEADS):
        ig_col = gates[:, h:h + 1]
        ig_row = gates_t[h:h + 1, :]
        lf_col = _log_sigmoid(gates[:, M_HEADS + h:M_HEADS + h + 1])
        lf_row = _log_sigmoid(gates_t[M_HEADS + h:M_HEADS + h + 1, :])
        b_col = jnp.sum(jnp.where(causal, lf_row, 0.0), axis=1, keepdims=True)
        b_row = jnp.sum(jnp.where(t_idx <= s_idx, lf_col, 0.0), axis=0, keepdims=True)
        m_prev = m_ref[h][:, 0:1]

        a_col = b_col + m_prev
        dmat = jnp.where(causal, b_col - b_row + ig_row, MASKED)
        m_rows = jnp.maximum(a_col, jnp.max(dmat, axis=1, keepdims=True))
        inter = jnp.exp(a_col - m_rows)

        q = q_ref[:, h * M_QK_DIM:(h + 1) * M_QK_DIM]
        k = k_ref[:, h * M_QK_DIM:(h + 1) * M_QK_DIM] * jnp.asarray(M_QK_DIM ** -0.5, BF16)
        v = v_ref[:, h * M_V_DIM:(h + 1) * M_V_DIM]
        c_prev = c_ref[h]
        n_prev = n_ref[h]

        qk = lax.dot_general(q, k, (((1,), (1,)), ((), ())), preferred_element_type=F32)
        w = qk * jnp.exp(dmat - m_rows)
        num = inter * jnp.dot(q, c_prev.astype(BF16), preferred_element_type=F32)
        num = num + jnp.dot(w.astype(BF16), v, preferred_element_type=F32)
        qn = jnp.sum(q.astype(F32) * n_prev, axis=1, keepdims=True)
        den = inter * qn + jnp.sum(w, axis=1, keepdims=True)
        hid = num * (1.0 / jnp.maximum(jnp.abs(den), jnp.exp(-m_rows)))
        og = og_ref[:, h * M_V_DIM:(h + 1) * M_V_DIM].astype(F32)
        o_ref[:, h * M_V_DIM:(h + 1) * M_V_DIM] = (hid * jax.nn.sigmoid(og)).astype(BF16)

        m_end = m_rows[L - 1:L, :]
        b_end = b_col[L - 1:L, :]
        decay = jnp.exp(b_end + m_prev - m_end)
        wk = jnp.exp(b_end - b_col + ig_col - m_end)
        kw = k.astype(F32) * wk
        c_ref[h] = decay * c_prev + lax.dot_general(
            kw.astype(BF16), v, (((0,), (0,)), ((), ())), preferred_element_type=F32)
        n_ref[h] = decay * n_prev + jnp.sum(kw, axis=0, keepdims=True)
        m_ref[h] = jnp.broadcast_to(m_end, (1, LANES))


def _mlstm_core(proj, gates, batch, seq):
    nc = seq // CHUNK
    qk_cols = M_HEADS * M_QK_DIM
    v_cols = M_HEADS * M_V_DIM
    row = lambda b, c: b * nc + c
    return pl.pallas_call(
        _mlstm_kernel,
        grid=(batch, nc),
        in_specs=[
            pl.BlockSpec((CHUNK, qk_cols), lambda b, c: (row(b, c), 0)),
            pl.BlockSpec((CHUNK, qk_cols), lambda b, c: (row(b, c), 1)),
            pl.BlockSpec((CHUNK, v_cols), lambda b, c: (row(b, c), 1)),
            pl.BlockSpec((CHUNK, v_cols), lambda b, c: (row(b, c), 2)),
            pl.BlockSpec((CHUNK, LANES), lambda b, c: (row(b, c), 0)),
        ],
        out_specs=pl.BlockSpec((CHUNK, v_cols), lambda b, c: (row(b, c), 0)),
        out_shape=jax.ShapeDtypeStruct((batch * seq, v_cols), BF16),
        scratch_shapes=[
            pltpu.VMEM((M_HEADS, M_QK_DIM, M_V_DIM), F32),
            pltpu.VMEM((M_HEADS, 1, M_QK_DIM), F32),
            pltpu.VMEM((M_HEADS, 1, LANES), F32),
        ],
        compiler_params=_compiler_params(("parallel", "arbitrary")),
        name="mlstm_core",
    )(proj, proj, proj, proj, gates)


def _mlstm_mixer(x, w_in, b_gates, w_o, g, b, batch, seq):
    w_main = w_in[:, :M_MAIN_COLS].astype(BF16)
    pad = LANES - M_GATE_COLS
    w_gates = jnp.pad(w_in[:, M_MAIN_COLS:], ((0, 0), (0, pad))).astype(BF16)
    bias = jnp.pad(b_gates.astype(F32), (0, pad)).reshape(1, LANES)
    proj, gates = _mlstm_proj(x, w_main, w_gates, bias)
    hid = _mlstm_core(proj, gates, batch, seq)
    return _outproj_ln(hid, w_o.astype(BF16), x, g, b)


def kernel(x, ffn_w1, ffn_w3, ffn_w2, ln_g, ln_b, att_w_qkv, att_sinks, att_w_o,
           mlstm_w_in, mlstm_b_gates, mlstm_w_o):
    batch, seq, d = x.shape
    assert d == D_MODEL and seq % WINDOW == 0 and (batch * seq) % FFN_TM == 0
    xt = x.reshape(batch * seq, d)
    w1 = ffn_w1.astype(BF16)
    w3 = ffn_w3.astype(BF16)
    w2 = ffn_w2.astype(BF16)
    g = ln_g.astype(F32).reshape(DEPTH, 3, 1, d)
    b = ln_b.astype(F32).reshape(DEPTH, 3, 1, d)
    for layer in range(DEPTH):
        xt = _ffn_ln(xt, w1, w3, w2, g[layer, 0], b[layer, 0], layer, 0)
        j = layer // 2
        if layer % 2 == 0:
            xt = _attention_mixer(xt, att_w_qkv[j], att_sinks[j], att_w_o[j],
                                  g[layer, 1], b[layer, 1], batch, seq)
        else:
            xt = _mlstm_mixer(xt, mlstm_w_in[j], mlstm_b_gates[j], mlstm_w_o[j],
                              g[layer, 1], b[layer, 1], batch, seq)
        xt = _ffn_ln(xt, w1, w3, w2, g[layer, 2], b[layer, 2], layer, 1)
    return xt.reshape(batch, seq, d).astype(x.dtype)
```

```python
import functools

import jax
import jax.numpy as jnp
from jax import lax
from jax.experimental import pallas as pl
from jax.experimental.pallas import tpu as pltpu

D_MODEL = 2048
DEPTH = 4
D_FF = 5632
ATT_HEADS = 32
ATT_KV_HEADS = 4
ATT_GROUP = ATT_HEADS // ATT_KV_HEADS
ATT_HEAD_DIM = D_MODEL // ATT_HEADS
WINDOW = 128
M_HEADS = 4
M_V_DIM = D_MODEL // M_HEADS
M_QK_DIM = M_V_DIM // 2
CHUNK = 128
DEEPNORM_ALPHA = (2.0 * DEPTH) ** 0.25
LN_EPS = 1e-5

LANES = 128
MASKED = -1e30
NO_KEY_DISTANCE = 1e30
VMEM_LIMIT_BYTES = 56 * 1024 * 1024

BF16 = jnp.bfloat16
F32 = jnp.float32


def _compiler_params(semantics):
    return pltpu.CompilerParams(dimension_semantics=semantics,
                                vmem_limit_bytes=VMEM_LIMIT_BYTES)


def _layer_norm(y, g, b):
    mu = jnp.mean(y, axis=-1, keepdims=True)
    yc = y - mu
    var = jnp.mean(yc * yc, axis=-1, keepdims=True)
    return yc * lax.rsqrt(var + LN_EPS) * g + b


FFN_TM = 1024
FFN_SUB = 512
FFN_TF = 512
LN_ROWS = 128


def _ffn_kernel(x_hbm, w1_ref, w3_ref, w2_ref, g_ref, b_ref, o_ref, xf_ref, xb_ref, sem):
    i = pl.program_id(0)
    f = pl.program_id(1)

    def x_copy(block):
        return pltpu.make_async_copy(x_hbm.at[pl.ds(block * FFN_TM, FFN_TM), :], xf_ref, sem)

    @pl.when(f == 0)
    def _():
        @pl.when(i == 0)
        def _():
            x_copy(0).start()

        x_copy(i).wait()
        for r in range(FFN_TM // FFN_SUB):
            rows = pl.ds(r * FFN_SUB, FFN_SUB)
            xf = xf_ref[rows, :]
            o_ref[rows, :] = DEEPNORM_ALPHA * xf
            xb_ref[rows, :] = xf.astype(BF16)

    @pl.when((f == 1) & (i + 1 < pl.num_programs(0)))
    def _():
        x_copy(i + 1).start()

    for r in range(FFN_TM // FFN_SUB):
        rows = pl.ds(r * FFN_SUB, FFN_SUB)
        xb = xb_ref[rows, :]
        h1 = jnp.dot(xb, w1_ref[...], preferred_element_type=F32)
        h3 = jnp.dot(xb, w3_ref[...], preferred_element_type=F32)
        act = (0.5 * (h1 * jax.nn.sigmoid(h1) * h3)).astype(BF16)
        o_ref[rows, :] += jnp.dot(act, w2_ref[...], preferred_element_type=F32)

    @pl.when(f == pl.num_programs(1) - 1)
    def _():
        for r in range(FFN_TM // LN_ROWS):
            rows = pl.ds(r * LN_ROWS, LN_ROWS)
            o_ref[rows, :] = _layer_norm(o_ref[rows, :], g_ref[...], b_ref[...])


def _ffn_ln(x, w1, w3, w2, g, b, layer, half):
    t = x.shape[0]
    grid = (t // FFN_TM, D_FF // FFN_TF)
    assert grid[1] >= 2
    return pl.pallas_call(
        _ffn_kernel,
        grid=grid,
        in_specs=[
            pl.BlockSpec(memory_space=pl.ANY),
            pl.BlockSpec((None, None, D_MODEL, FFN_TF), lambda i, f: (layer, half, 0, f)),
            pl.BlockSpec((None, None, D_MODEL, FFN_TF), lambda i, f: (layer, half, 0, f)),
            pl.BlockSpec((None, None, FFN_TF, D_MODEL), lambda i, f: (layer, half, f, 0)),
            pl.BlockSpec((1, D_MODEL), lambda i, f: (0, 0)),
            pl.BlockSpec((1, D_MODEL), lambda i, f: (0, 0)),
        ],
        out_specs=pl.BlockSpec((FFN_TM, D_MODEL), lambda i, f: (i, 0)),
        out_shape=jax.ShapeDtypeStruct((t, D_MODEL), F32),
        scratch_shapes=[pltpu.VMEM((FFN_TM, D_MODEL), F32),
                        pltpu.VMEM((FFN_TM, D_MODEL), BF16),
                        pltpu.SemaphoreType.DMA(())],
        compiler_params=_compiler_params(("arbitrary", "arbitrary")),
        name="ffn_ln",
    )(x, w1, w3, w2, g, b)


OUT_TM = 512


def _outproj_kernel(a_ref, w_ref, x_ref, g_ref, b_ref, o_ref):
    y = jnp.dot(a_ref[...], w_ref[...], preferred_element_type=F32)
    y = DEEPNORM_ALPHA * x_ref[...] + y
    o_ref[...] = _layer_norm(y, g_ref[...], b_ref[...])


def _outproj_ln(a, w, x, g, b):
    t = x.shape[0]
    return pl.pallas_call(
        _outproj_kernel,
        grid=(t // OUT_TM,),
        in_specs=[
            pl.BlockSpec((OUT_TM, D_MODEL), lambda i: (i, 0)),
            pl.BlockSpec((D_MODEL, D_MODEL), lambda i: (0, 0)),
            pl.BlockSpec((OUT_TM, D_MODEL), lambda i: (i, 0)),
            pl.BlockSpec((1, D_MODEL), lambda i: (0, 0)),
            pl.BlockSpec((1, D_MODEL), lambda i: (0, 0)),
        ],
        out_specs=pl.BlockSpec((OUT_TM, D_MODEL), lambda i: (i, 0)),
        out_shape=jax.ShapeDtypeStruct((t, D_MODEL), F32),
        compiler_params=_compiler_params(("parallel",)),
        name="outproj_ln",
    )(a, w, x, g, b)


QKV_TM = 512
ATT_KV_COLS = ATT_KV_HEADS * ATT_HEAD_DIM
ATT_PAIRS = ATT_HEADS // 2


def _qkv_kernel(x_ref, wq_ref, wk_ref, wv_ref, q_ref, k_ref, v_ref):
    xb = x_ref[...].astype(BF16)
    q_ref[...] = jnp.dot(xb, wq_ref[...], preferred_element_type=F32).astype(BF16)
    k_ref[...] = jnp.dot(xb, wk_ref[...], preferred_element_type=F32).astype(BF16)
    v_ref[...] = jnp.dot(xb, wv_ref[...], preferred_element_type=F32).astype(BF16)


def _qkv_proj(x, wq, wk, wv):
    t = x.shape[0]
    return pl.pallas_call(
        _qkv_kernel,
        grid=(t // QKV_TM,),
        in_specs=[
            pl.BlockSpec((QKV_TM, D_MODEL), lambda i: (i, 0)),
            pl.BlockSpec((D_MODEL, D_MODEL), lambda i: (0, 0)),
            pl.BlockSpec((D_MODEL, ATT_KV_COLS), lambda i: (0, 0)),
            pl.BlockSpec((D_MODEL, ATT_KV_COLS), lambda i: (0, 0)),
        ],
        out_specs=[
            pl.BlockSpec((QKV_TM, D_MODEL), lambda i: (i, 0)),
            pl.BlockSpec((QKV_TM, ATT_KV_COLS), lambda i: (i, 0)),
            pl.BlockSpec((QKV_TM, ATT_KV_COLS), lambda i: (i, 0)),
        ],
        out_shape=[
            jax.ShapeDtypeStruct((t, D_MODEL), BF16),
            jax.ShapeDtypeStruct((t, ATT_KV_COLS), BF16),
            jax.ShapeDtypeStruct((t, ATT_KV_COLS), BF16),
        ],
        compiler_params=_compiler_params(("parallel",)),
        name="qkv_proj",
    )(x, wq, wk, wv)


def _alibi_slope(head):
    return 2.0 ** (-8.0 * (head + 1) / ATT_HEADS)


def _attn_kernel(sink_ref, q_ref, kc_ref, kp_ref, vc_ref, vp_ref, o_ref):
    n = pl.program_id(1)
    W = WINDOW
    rows = 2 * W
    qi = lax.broadcasted_iota(jnp.int32, (rows, W), 0) & (W - 1)
    j = lax.broadcasted_iota(jnp.int32, (rows, W), 1)
    from_prev = j > qi
    dist = jnp.where(from_prev, W + qi - j, qi - j).astype(F32)
    dist = jnp.where(from_prev & (n == 0), NO_KEY_DISTANCE, dist)
    is_lo_row = lax.broadcasted_iota(jnp.int32, (rows, 1), 0) < W
    lo_lane = lax.broadcasted_iota(jnp.int32, (1, LANES), 1) < ATT_HEAD_DIM
    zero = jnp.zeros((), BF16)

    for c2 in range(ATT_KV_HEADS // 2):
        cols = slice(LANES * c2, LANES * (c2 + 1))
        kk = jnp.concatenate([kp_ref[:, cols], kc_ref[:, cols]], axis=0)
        vv = jnp.concatenate([vp_ref[:, cols], vc_ref[:, cols]], axis=0)
        vcat = jnp.concatenate([jnp.where(lo_lane, vv, zero),
                                jnp.where(lo_lane, zero, vv)], axis=0)
        for g in range(ATT_GROUP):
            pair = c2 * ATT_GROUP + g
            head_lo = (2 * c2) * ATT_GROUP + g
            head_hi = (2 * c2 + 1) * ATT_GROUP + g
            pcols = slice(LANES * pair, LANES * (pair + 1))
            qp = q_ref[:, pcols] * jnp.asarray(ATT_HEAD_DIM ** -0.5, BF16)
            qs = jnp.concatenate([jnp.where(lo_lane, qp, zero),
                                  jnp.where(lo_lane, zero, qp)], axis=0)
            s2 = lax.dot_general(qs, kk, (((1,), (1,)), ((), ())),
                                 preferred_element_type=F32)
            slope = jnp.where(is_lo_row, _alibi_slope(head_lo), _alibi_slope(head_hi))
            sink = jnp.where(is_lo_row, sink_ref[head_lo], sink_ref[head_hi])
            s = jnp.where(from_prev, s2[:, :W], s2[:, W:]) - slope * dist
            m = jnp.maximum(jnp.max(s, axis=-1, keepdims=True), sink)
            p = jnp.exp(s - m)
            denom = jnp.sum(p, axis=-1, keepdims=True) + jnp.exp(sink - m)
            p = p * (1.0 / denom)
            p_prev = jnp.where(from_prev, p, 0.0).astype(BF16)
            p_cur = jnp.where(from_prev, 0.0, p).astype(BF16)
            pcat = jnp.concatenate([p_prev[:W], p_cur[:W], p_prev[W:], p_cur[W:]], axis=1)
            o = jnp.dot(pcat, vcat, preferred_element_type=F32)
            o_ref[:, pcols] = o.astype(BF16)


def _attn_core(sinks, q, k, v, batch, seq):
    nb = seq // WINDOW
    cur = lambda b, n: (b * nb + n, 0)
    prev = lambda b, n: (b * nb + jnp.maximum(n - 1, 0), 0)
    return pl.pallas_call(
        _attn_kernel,
        grid=(batch, nb),
        in_specs=[
            pl.BlockSpec(memory_space=pltpu.SMEM),
            pl.BlockSpec((WINDOW, D_MODEL), cur),
            pl.BlockSpec((WINDOW, ATT_KV_COLS), cur),
            pl.BlockSpec((WINDOW, ATT_KV_COLS), prev),
            pl.BlockSpec((WINDOW, ATT_KV_COLS), cur),
            pl.BlockSpec((WINDOW, ATT_KV_COLS), prev),
        ],
        out_specs=pl.BlockSpec((WINDOW, D_MODEL), cur),
        out_shape=jax.ShapeDtypeStruct((batch * seq, D_MODEL), BF16),
        compiler_params=_compiler_params(("parallel", "arbitrary")),
        name="swa_core",
    )(sinks, q, k, k, v, v)


def _head_pair_permutation():
    perm = []
    for c2 in range(ATT_KV_HEADS // 2):
        for g in range(ATT_GROUP):
            for half in range(2):
                head = (2 * c2 + half) * ATT_GROUP + g
                perm.extend(range(head * ATT_HEAD_DIM, (head + 1) * ATT_HEAD_DIM))
    return jnp.asarray(perm, jnp.int32)


def _attention_mixer(x, w_qkv, sinks, w_o, g, b, batch, seq):
    perm = _head_pair_permutation()
    nq = ATT_HEADS * ATT_HEAD_DIM
    wq = w_qkv[:, :nq][:, perm].astype(BF16)
    wk = w_qkv[:, nq:nq + ATT_KV_COLS].astype(BF16)
    wv = w_qkv[:, nq + ATT_KV_COLS:].astype(BF16)
    wo = w_o[perm, :].astype(BF16)
    q, k, v = _qkv_proj(x, wq, wk, wv)
    o = _attn_core(sinks.astype(F32), q, k, v, batch, seq)
    return _outproj_ln(o, wo, x, g, b)


MPROJ_TM = 1024
MPROJ_TN = 1024
M_MAIN_COLS = 2 * M_HEADS * M_QK_DIM + 2 * M_HEADS * M_V_DIM
M_GATE_COLS = 2 * M_HEADS


def _mproj_kernel(x_ref, w_ref, wg_ref, bg_ref, p_ref, gates_ref, xb_ref):
    j = pl.program_id(1)

    @pl.when(j == 0)
    def _():
        xb = x_ref[...].astype(BF16)
        xb_ref[...] = xb
        gates_ref[...] = jnp.dot(xb, wg_ref[...], preferred_element_type=F32) + bg_ref[...]

    p_ref[...] = jnp.dot(xb_ref[...], w_ref[...], preferred_element_type=F32).astype(BF16)


def _mlstm_proj(x, w_main, w_gates, b_gates):
    t = x.shape[0]
    return pl.pallas_call(
        _mproj_kernel,
        grid=(t // MPROJ_TM, M_MAIN_COLS // MPROJ_TN),
        in_specs=[
            pl.BlockSpec((MPROJ_TM, D_MODEL), lambda i, j: (i, 0)),
            pl.BlockSpec((D_MODEL, MPROJ_TN), lambda i, j: (0, j)),
            pl.BlockSpec((D_MODEL, LANES), lambda i, j: (0, 0)),
            pl.BlockSpec((1, LANES), lambda i, j: (0, 0)),
        ],
        out_specs=[
            pl.BlockSpec((MPROJ_TM, MPROJ_TN), lambda i, j: (i, j)),
            pl.BlockSpec((MPROJ_TM, LANES), lambda i, j: (i, 0)),
        ],
        out_shape=[
            jax.ShapeDtypeStruct((t, M_MAIN_COLS), BF16),
            jax.ShapeDtypeStruct((t, LANES), F32),
        ],
        scratch_shapes=[pltpu.VMEM((MPROJ_TM, D_MODEL), BF16)],
        compiler_params=_compiler_params(("parallel", "arbitrary")),
        name="mlstm_proj",
    )(x, w_main, w_gates, b_gates)


def _log_sigmoid(x):
    return jnp.minimum(x, 0.0) - jnp.log1p(jnp.exp(-jnp.abs(x)))


def _mlstm_kernel(q_ref, k_ref, v_ref, og_ref, gates_ref, o_ref, c_ref, n_ref, m_ref):
    L = CHUNK

    @pl.when(pl.program_id(1) == 0)
    def _():
        c_ref[...] = jnp.zeros_like(c_ref)
        n_ref[...] = jnp.zeros_like(n_ref)
        m_ref[...] = jnp.zeros_like(m_ref)

    gates = gates_ref[...]
    gates_t = gates.T
    t_idx = lax.broadcasted_iota(jnp.int32, (L, L), 0)
    s_idx = lax.broadcasted_iota(jnp.int32, (L, L), 1)
    causal = s_idx <= t_idx

    for h in range(M_HEADS):
        ig_col = gates[:, h:h + 1]
        ig_row = gates_t[h:h + 1, :]
        lf_col = _log_sigmoid(gates[:, M_HEADS + h:M_HEADS + h + 1])
        lf_row = _log_sigmoid(gates_t[M_HEADS + h:M_HEADS + h + 1, :])
        b_col = jnp.sum(jnp.where(causal, lf_row, 0.0), axis=1, keepdims=True)
        b_row = jnp.sum(jnp.where(t_idx <= s_idx, lf_col, 0.0), axis=0, keepdims=True)
        m_prev = m_ref[h][:, 0:1]

        a_col = b_col + m_prev
        dmat = jnp.where(causal, b_col - b_row + ig_row, MASKED)
        m_rows = jnp.maximum(a_col, jnp.max(dmat, axis=1, keepdims=True))
        inter = jnp.exp(a_col - m_rows)

        q = q_ref[:, h * M_QK_DIM:(h + 1) * M_QK_DIM]
        k = k_ref[:, h * M_QK_DIM:(h + 1) * M_QK_DIM] * jnp.asarray(M_QK_DIM ** -0.5, BF16)
        v = v_ref[:, h * M_V_DIM:(h + 1) * M_V_DIM]
        c_prev = c_ref[h]
        n_prev = n_ref[h]

        qk = lax.dot_general(q, k, (((1,), (1,)), ((), ())), preferred_element_type=F32)
        w = qk * jnp.exp(dmat - m_rows)
        num = inter * jnp.dot(q, c_prev.astype(BF16), preferred_element_type=F32)
        num = num + jnp.dot(w.astype(BF16), v, preferred_element_type=F32)
        qn = jnp.sum(q.astype(F32) * n_prev, axis=1, keepdims=True)
        den = inter * qn + jnp.sum(w, axis=1, keepdims=True)
        hid = num * (1.0 / jnp.maximum(jnp.abs(den), jnp.exp(-m_rows)))
        og = og_ref[:, h * M_V_DIM:(h + 1) * M_V_DIM].astype(F32)
        o_ref[:, h * M_V_DIM:(h + 1) * M_V_DIM] = (hid * jax.nn.sigmoid(og)).astype(BF16)

        m_end = m_rows[L - 1:L, :]
        b_end = b_col[L - 1:L, :]
        decay = jnp.exp(b_end + m_prev - m_end)
        wk = jnp.exp(b_end - b_col + ig_col - m_end)
        kw = k.astype(F32) * wk
        c_ref[h] = decay * c_prev + lax.dot_general(
            kw.astype(BF16), v, (((0,), (0,)), ((), ())), preferred_element_type=F32)
        n_ref[h] = decay * n_prev + jnp.sum(kw, axis=0, keepdims=True)
        m_ref[h] = jnp.broadcast_to(m_end, (1, LANES))


def _mlstm_core(proj, gates, batch, seq):
    nc = seq // CHUNK
    qk_cols = M_HEADS * M_QK_DIM
    v_cols = M_HEADS * M_V_DIM
    row = lambda b, c: b * nc + c
    return pl.pallas_call(
        _mlstm_kernel,
        grid=(batch, nc),
        in_specs=[
            pl.BlockSpec((CHUNK, qk_cols), lambda b, c: (row(b, c), 0)),
            pl.BlockSpec((CHUNK, qk_cols), lambda b, c: (row(b, c), 1)),
            pl.BlockSpec((CHUNK, v_cols), lambda b, c: (row(b, c), 1)),
            pl.BlockSpec((CHUNK, v_cols), lambda b, c: (row(b, c), 2)),
            pl.BlockSpec((CHUNK, LANES), lambda b, c: (row(b, c), 0)),
        ],
        out_specs=pl.BlockSpec((CHUNK, v_cols), lambda b, c: (row(b, c), 0)),
        out_shape=jax.ShapeDtypeStruct((batch * seq, v_cols), BF16),
        scratch_shapes=[
            pltpu.VMEM((M_HEADS, M_QK_DIM, M_V_DIM), F32),
            pltpu.VMEM((M_HEADS, 1, M_QK_DIM), F32),
            pltpu.VMEM((M_HEADS, 1, LANES), F32),
        ],
        compiler_params=_compiler_params(("parallel", "arbitrary")),
        name="mlstm_core",
    )(proj, proj, proj, proj, gates)


def _mlstm_mixer(x, w_in, b_gates, w_o, g, b, batch, seq):
    w_main = w_in[:, :M_MAIN_COLS].astype(BF16)
    pad = LANES - M_GATE_COLS
    w_gates = jnp.pad(w_in[:, M_MAIN_COLS:], ((0, 0), (0, pad))).astype(BF16)
    bias = jnp.pad(b_gates.astype(F32), (0, pad)).reshape(1, LANES)
    proj, gates = _mlstm_proj(x, w_main, w_gates, bias)
    hid = _mlstm_core(proj, gates, batch, seq)
    return _outproj_ln(hid, w_o.astype(BF16), x, g, b)


def kernel(x, ffn_w1, ffn_w3, ffn_w2, ln_g, ln_b, att_w_qkv, att_sinks, att_w_o,
           mlstm_w_in, mlstm_b_gates, mlstm_w_o):
    batch, seq, d = x.shape
    assert d == D_MODEL and seq % WINDOW == 0 and (batch * seq) % FFN_TM == 0
    xt = x.reshape(batch * seq, d)
    w1 = ffn_w1.astype(BF16)
    w3 = ffn_w3.astype(BF16)
    w2 = ffn_w2.astype(BF16)
    g = ln_g.astype(F32).reshape(DEPTH, 3, 1, d)
    b = ln_b.astype(F32).reshape(DEPTH, 3, 1, d)
    for layer in range(DEPTH):
        xt = _ffn_ln(xt, w1, w3, w2, g[layer, 0], b[layer, 0], layer, 0)
        j = layer // 2
        if layer % 2 == 0:
            xt = _attention_mixer(xt, att_w_qkv[j], att_sinks[j], att_w_o[j],
                                  g[layer, 1], b[layer, 1], batch, seq)
        else:
            xt = _mlstm_mixer(xt, mlstm_w_in[j], mlstm_b_gates[j], mlstm_w_o[j],
                              g[layer, 1], b[layer, 1], batch, seq)
        xt = _ffn_ln(xt, w1, w3, w2, g[layer, 2], b[layer, 2], layer, 1)
    return xt.reshape(batch, seq, d).astype(x.dtype)
```

```python
import functools

import jax
import jax.numpy as jnp
from jax import lax
from jax.experimental import pallas as pl
from jax.experimental.pallas import tpu as pltpu

D_MODEL = 2048
DEPTH = 4
D_FF = 5632
ATT_HEADS = 32
ATT_KV_HEADS = 4
ATT_GROUP = ATT_HEADS // ATT_KV_HEADS
ATT_HEAD_DIM = D_MODEL // ATT_HEADS
WINDOW = 128
M_HEADS = 4
M_V_DIM = D_MODEL // M_HEADS
M_QK_DIM = M_V_DIM // 2
CHUNK = 128
DEEPNORM_ALPHA = (2.0 * DEPTH) ** 0.25
LN_EPS = 1e-5

LANES = 128
MASKED = -1e30
NO_KEY_DISTANCE = 1e30
VMEM_LIMIT_BYTES = 56 * 1024 * 1024

BF16 = jnp.bfloat16
F32 = jnp.float32


def _compiler_params(semantics):
    return pltpu.CompilerParams(dimension_semantics=semantics,
                                vmem_limit_bytes=VMEM_LIMIT_BYTES)


def _layer_norm(y, g, b):
    mu = jnp.mean(y, axis=-1, keepdims=True)
    yc = y - mu
    var = jnp.mean(yc * yc, axis=-1, keepdims=True)
    return yc * lax.rsqrt(var + LN_EPS) * g + b


FFN_TM = 1024
FFN_SUB = 512
FFN_TF = 512
LN_ROWS = 128


def _ffn_kernel(x_hbm, w1_ref, w3_ref, w2_ref, g_ref, b_ref, o_ref, xf_ref, xb_ref, sem):
    i = pl.program_id(0)
    f = pl.program_id(1)

    def x_copy(block):
        return pltpu.make_async_copy(x_hbm.at[pl.ds(block * FFN_TM, FFN_TM), :], xf_ref, sem)

    last = pl.num_programs(1) - 1

    def step(is_first, is_last):
        for r in range(FFN_TM // FFN_SUB):
            rows = pl.ds(r * FFN_SUB, FFN_SUB)
            if is_first:
                xf = xf_ref[rows, :]
                xb = xf.astype(BF16)
                xb_ref[rows, :] = xb
                acc = DEEPNORM_ALPHA * xf
            else:
                xb = xb_ref[rows, :]
                acc = o_ref[rows, :]
            h1 = jnp.dot(xb, w1_ref[...], preferred_element_type=F32)
            h3 = jnp.dot(xb, w3_ref[...], preferred_element_type=F32)
            act = (0.5 * (h1 * jax.nn.sigmoid(h1) * h3)).astype(BF16)
            o_ref[rows, :] = acc + jnp.dot(act, w2_ref[...], preferred_element_type=F32)
            if is_last:
                for c in range(FFN_SUB // LN_ROWS):
                    ln_rows = pl.ds(r * FFN_SUB + c * LN_ROWS, LN_ROWS)
                    o_ref[ln_rows, :] = _layer_norm(o_ref[ln_rows, :], g_ref[...], b_ref[...])

    @pl.when(f == 0)
    def _():
        @pl.when(i == 0)
        def _():
            x_copy(0).start()

        x_copy(i).wait()
        step(True, False)

    @pl.when(f == 1)
    def _():
        @pl.when(i + 1 < pl.num_programs(0))
        def _():
            x_copy(i + 1).start()

    @pl.when((f > 0) & (f < last))
    def _():
        step(False, False)

    @pl.when(f == last)
    def _():
        step(False, True)


def _ffn_ln(x, w1, w3, w2, g, b, layer, half):
    t = x.shape[0]
    grid = (t // FFN_TM, D_FF // FFN_TF)
    assert grid[1] >= 2
    return pl.pallas_call(
        _ffn_kernel,
        grid=grid,
        in_specs=[
            pl.BlockSpec(memory_space=pl.ANY),
            pl.BlockSpec((None, None, D_MODEL, FFN_TF), lambda i, f: (layer, half, 0, f)),
            pl.BlockSpec((None, None, D_MODEL, FFN_TF), lambda i, f: (layer, half, 0, f)),
            pl.BlockSpec((None, None, FFN_TF, D_MODEL), lambda i, f: (layer, half, f, 0)),
            pl.BlockSpec((1, D_MODEL), lambda i, f: (0, 0)),
            pl.BlockSpec((1, D_MODEL), lambda i, f: (0, 0)),
        ],
        out_specs=pl.BlockSpec((FFN_TM, D_MODEL), lambda i, f: (i, 0)),
        out_shape=jax.ShapeDtypeStruct((t, D_MODEL), F32),
        scratch_shapes=[pltpu.VMEM((FFN_TM, D_MODEL), F32),
                        pltpu.VMEM((FFN_TM, D_MODEL), BF16),
                        pltpu.SemaphoreType.DMA(())],
        compiler_params=_compiler_params(("arbitrary", "arbitrary")),
        name="ffn_ln",
    )(x, w1, w3, w2, g, b)


OUT_TM = 512


OUT_SUB = 256


def _outproj_kernel(a_ref, w_ref, x_ref, g_ref, b_ref, o_ref):
    parts = [pl.ds(r * OUT_SUB, OUT_SUB) for r in range(OUT_TM // OUT_SUB)]
    ys = [jnp.dot(a_ref[rows, :], w_ref[...], preferred_element_type=F32) for rows in parts]
    for rows, y in zip(parts, ys):
        y = DEEPNORM_ALPHA * x_ref[rows, :] + y
        o_ref[rows, :] = _layer_norm(y, g_ref[...], b_ref[...])


def _outproj_ln(a, w, x, g, b):
    t = x.shape[0]
    return pl.pallas_call(
        _outproj_kernel,
        grid=(t // OUT_TM,),
        in_specs=[
            pl.BlockSpec((OUT_TM, D_MODEL), lambda i: (i, 0)),
            pl.BlockSpec((D_MODEL, D_MODEL), lambda i: (0, 0)),
            pl.BlockSpec((OUT_TM, D_MODEL), lambda i: (i, 0)),
            pl.BlockSpec((1, D_MODEL), lambda i: (0, 0)),
            pl.BlockSpec((1, D_MODEL), lambda i: (0, 0)),
        ],
        out_specs=pl.BlockSpec((OUT_TM, D_MODEL), lambda i: (i, 0)),
        out_shape=jax.ShapeDtypeStruct((t, D_MODEL), F32),
        compiler_params=_compiler_params(("parallel",)),
        name="outproj_ln",
    )(a, w, x, g, b)


QKV_TM = 512
ATT_KV_COLS = ATT_KV_HEADS * ATT_HEAD_DIM
ATT_PAIRS = ATT_HEADS // 2


def _qkv_kernel(x_ref, wq_ref, wk_ref, wv_ref, q_ref, k_ref, v_ref):
    xb = x_ref[...].astype(BF16)
    q_ref[...] = jnp.dot(xb, wq_ref[...], preferred_element_type=F32).astype(BF16)
    k_ref[...] = jnp.dot(xb, wk_ref[...], preferred_element_type=F32).astype(BF16)
    v_ref[...] = jnp.dot(xb, wv_ref[...], preferred_element_type=F32).astype(BF16)


def _qkv_proj(x, wq, wk, wv):
    t = x.shape[0]
    return pl.pallas_call(
        _qkv_kernel,
        grid=(t // QKV_TM,),
        in_specs=[
            pl.BlockSpec((QKV_TM, D_MODEL), lambda i: (i, 0)),
            pl.BlockSpec((D_MODEL, D_MODEL), lambda i: (0, 0)),
            pl.BlockSpec((D_MODEL, ATT_KV_COLS), lambda i: (0, 0)),
            pl.BlockSpec((D_MODEL, ATT_KV_COLS), lambda i: (0, 0)),
        ],
        out_specs=[
            pl.BlockSpec((QKV_TM, D_MODEL), lambda i: (i, 0)),
            pl.BlockSpec((QKV_TM, ATT_KV_COLS), lambda i: (i, 0)),
            pl.BlockSpec((QKV_TM, ATT_KV_COLS), lambda i: (i, 0)),
        ],
        out_shape=[
            jax.ShapeDtypeStruct((t, D_MODEL), BF16),
            jax.ShapeDtypeStruct((t, ATT_KV_COLS), BF16),
            jax.ShapeDtypeStruct((t, ATT_KV_COLS), BF16),
        ],
        compiler_params=_compiler_params(("parallel",)),
        name="qkv_proj",
    )(x, wq, wk, wv)


def _alibi_slope(head):
    return 2.0 ** (-8.0 * (head + 1) / ATT_HEADS)


def _attn_kernel(sink_ref, q_ref, kc_ref, kp_ref, vc_ref, vp_ref, o_ref):
    n = pl.program_id(1)
    W = WINDOW
    rows = 2 * W
    qi = lax.broadcasted_iota(jnp.int32, (rows, W), 0) & (W - 1)
    j = lax.broadcasted_iota(jnp.int32, (rows, W), 1)
    from_prev = j > qi
    dist = jnp.where(from_prev, W + qi - j, qi - j).astype(F32)
    dist = jnp.where(from_prev & (n == 0), NO_KEY_DISTANCE, dist)
    is_lo_row = lax.broadcasted_iota(jnp.int32, (rows, 1), 0) < W
    lo_lane = lax.broadcasted_iota(jnp.int32, (1, LANES), 1) < ATT_HEAD_DIM
    zero = jnp.zeros((), BF16)

    for c2 in range(ATT_KV_HEADS // 2):
        cols = slice(LANES * c2, LANES * (c2 + 1))
        kk = jnp.concatenate([kp_ref[:, cols], kc_ref[:, cols]], axis=0)
        vv = jnp.concatenate([vp_ref[:, cols], vc_ref[:, cols]], axis=0)
        vcat = jnp.concatenate([jnp.where(lo_lane, vv, zero),
                                jnp.where(lo_lane, zero, vv)], axis=0)
        G = range(ATT_GROUP)
        pcols = [slice(LANES * (c2 * ATT_GROUP + g), LANES * (c2 * ATT_GROUP + g + 1)) for g in G]
        head_lo = [(2 * c2) * ATT_GROUP + g for g in G]
        head_hi = [(2 * c2 + 1) * ATT_GROUP + g for g in G]
        slope = [jnp.where(is_lo_row, _alibi_slope(head_lo[g]), _alibi_slope(head_hi[g])) for g in G]
        sink = [jnp.where(is_lo_row, sink_ref[head_lo[g]], sink_ref[head_hi[g]]) for g in G]
        qp = [q_ref[:, pcols[g]] * jnp.asarray(ATT_HEAD_DIM ** -0.5, BF16) for g in G]
        qs = [jnp.concatenate([jnp.where(lo_lane, qp[g], zero),
                               jnp.where(lo_lane, zero, qp[g])], axis=0) for g in G]
        s2 = [lax.dot_general(qs[g], kk, (((1,), (1,)), ((), ())),
                              preferred_element_type=F32) for g in G]
        s = [jnp.where(from_prev, s2[g][:, :W], s2[g][:, W:]) - slope[g] * dist for g in G]
        m = [jnp.maximum(jnp.max(s[g], axis=-1, keepdims=True), sink[g]) for g in G]
        p = [jnp.exp(s[g] - m[g]) for g in G]
        denom = [jnp.sum(p[g], axis=-1, keepdims=True) + jnp.exp(sink[g] - m[g]) for g in G]
        p = [p[g] * (1.0 / denom[g]) for g in G]
        for g in G:
            p_prev = jnp.where(from_prev, p[g], 0.0).astype(BF16)
            p_cur = jnp.where(from_prev, 0.0, p[g]).astype(BF16)
            pcat = jnp.concatenate([p_prev[:W], p_cur[:W], p_prev[W:], p_cur[W:]], axis=1)
            o = jnp.dot(pcat, vcat, preferred_element_type=F32)
            o_ref[:, pcols[g]] = o.astype(BF16)


def _attn_core(sinks, q, k, v, batch, seq):
    nb = seq // WINDOW
    cur = lambda b, n: (b * nb + n, 0)
    prev = lambda b, n: (b * nb + jnp.maximum(n - 1, 0), 0)
    return pl.pallas_call(
        _attn_kernel,
        grid=(batch, nb),
        in_specs=[
            pl.BlockSpec(memory_space=pltpu.SMEM),
            pl.BlockSpec((WINDOW, D_MODEL), cur),
            pl.BlockSpec((WINDOW, ATT_KV_COLS), cur),
            pl.BlockSpec((WINDOW, ATT_KV_COLS), prev),
            pl.BlockSpec((WINDOW, ATT_KV_COLS), cur),
            pl.BlockSpec((WINDOW, ATT_KV_COLS), prev),
        ],
        out_specs=pl.BlockSpec((WINDOW, D_MODEL), cur),
        out_shape=jax.ShapeDtypeStruct((batch * seq, D_MODEL), BF16),
        compiler_params=_compiler_params(("parallel", "arbitrary")),
        name="swa_core",
    )(sinks, q, k, k, v, v)


def _head_pair_permutation():
    perm = []
    for c2 in range(ATT_KV_HEADS // 2):
        for g in range(ATT_GROUP):
            for half in range(2):
                head = (2 * c2 + half) * ATT_GROUP + g
                perm.extend(range(head * ATT_HEAD_DIM, (head + 1) * ATT_HEAD_DIM))
    return jnp.asarray(perm, jnp.int32)


def _attention_mixer(x, w_qkv, sinks, w_o, g, b, batch, seq):
    perm = _head_pair_permutation()
    nq = ATT_HEADS * ATT_HEAD_DIM
    wq = w_qkv[:, :nq][:, perm].astype(BF16)
    wk = w_qkv[:, nq:nq + ATT_KV_COLS].astype(BF16)
    wv = w_qkv[:, nq + ATT_KV_COLS:].astype(BF16)
    wo = w_o[perm, :].astype(BF16)
    q, k, v = _qkv_proj(x, wq, wk, wv)
    o = _attn_core(sinks.astype(F32), q, k, v, batch, seq)
    return _outproj_ln(o, wo, x, g, b)


MPROJ_TM = 1024
MPROJ_TN = 1024
M_MAIN_COLS = 2 * M_HEADS * M_QK_DIM + 2 * M_HEADS * M_V_DIM
M_GATE_COLS = 2 * M_HEADS


def _mproj_kernel(x_ref, w_ref, wg_ref, bg_ref, p_ref, gates_ref, xb_ref):
    j = pl.program_id(1)

    @pl.when(j == 0)
    def _():
        xb = x_ref[...].astype(BF16)
        xb_ref[...] = xb
        gates_ref[...] = jnp.dot(xb, wg_ref[...], preferred_element_type=F32) + bg_ref[...]

    p_ref[...] = jnp.dot(xb_ref[...], w_ref[...], preferred_element_type=F32).astype(BF16)


def _mlstm_proj(x, w_main, w_gates, b_gates):
    t = x.shape[0]
    return pl.pallas_call(
        _mproj_kernel,
        grid=(t // MPROJ_TM, M_MAIN_COLS // MPROJ_TN),
        in_specs=[
            pl.BlockSpec((MPROJ_TM, D_MODEL), lambda i, j: (i, 0)),
            pl.BlockSpec((D_MODEL, MPROJ_TN), lambda i, j: (0, j)),
            pl.BlockSpec((D_MODEL, LANES), lambda i, j: (0, 0)),
            pl.BlockSpec((1, LANES), lambda i, j: (0, 0)),
        ],
        out_specs=[
            pl.BlockSpec((MPROJ_TM, MPROJ_TN), lambda i, j: (i, j)),
            pl.BlockSpec((MPROJ_TM, LANES), lambda i, j: (i, 0)),
        ],
        out_shape=[
            jax.ShapeDtypeStruct((t, M_MAIN_COLS), BF16),
            jax.ShapeDtypeStruct((t, LANES), F32),
        ],
        scratch_shapes=[pltpu.VMEM((MPROJ_TM, D_MODEL), BF16)],
        compiler_params=_compiler_params(("parallel", "arbitrary")),
        name="mlstm_proj",
    )(x, w_main, w_gates, b_gates)


def _log_sigmoid(x):
    return jnp.minimum(x, 0.0) - jnp.log1p(jnp.exp(-jnp.abs(x)))


M_BATCH = 2


def _mlstm_kernel(q_ref, k_ref, v_ref, og_ref, gates_ref, o_ref, c_ref, n_ref, m_ref):
    L = CHUNK

    @pl.when(pl.program_id(1) == 0)
    def _():
        c_ref[...] = jnp.zeros_like(c_ref)
        n_ref[...] = jnp.zeros_like(n_ref)
        m_ref[...] = jnp.zeros_like(m_ref)

    t_idx = lax.broadcasted_iota(jnp.int32, (L, L), 0)
    s_idx = lax.broadcasted_iota(jnp.int32, (L, L), 1)
    causal = s_idx <= t_idx

    units = [(bi, h) for bi in range(M_BATCH) for h in range(M_HEADS)]
    qk_cols = [slice(h * M_QK_DIM, (h + 1) * M_QK_DIM) for _, h in units]
    v_cols = [slice(h * M_V_DIM, (h + 1) * M_V_DIM) for _, h in units]
    gates = [gates_ref[bi] for bi in range(M_BATCH)]
    gates_t = [gt.T for gt in gates]

    ig_col, b_col, dmat, row_max = [], [], [], []
    for bi, h in units:
        ig_col.append(gates[bi][:, h:h + 1])
        ig_row = gates_t[bi][h:h + 1, :]
        lf_col = _log_sigmoid(gates[bi][:, M_HEADS + h:M_HEADS + h + 1])
        lf_row = _log_sigmoid(gates_t[bi][M_HEADS + h:M_HEADS + h + 1, :])
        bc = jnp.sum(jnp.where(causal, lf_row, 0.0), axis=1, keepdims=True)
        br = jnp.sum(jnp.where(t_idx <= s_idx, lf_col, 0.0), axis=0, keepdims=True)
        d = jnp.where(causal, bc - br + ig_row, MASKED)
        b_col.append(bc)
        dmat.append(d)
        row_max.append(jnp.max(d, axis=1, keepdims=True))

    q = [q_ref[bi, :, qk_cols[u]] for u, (bi, _) in enumerate(units)]
    k = [k_ref[bi, :, qk_cols[u]] * jnp.asarray(M_QK_DIM ** -0.5, BF16)
         for u, (bi, _) in enumerate(units)]
    qk = [lax.dot_general(q[u], k[u], (((1,), (1,)), ((), ())), preferred_element_type=F32)
          for u in range(len(units))]
    qc = [jnp.dot(q[u], c_ref[bi, h].astype(BF16), preferred_element_type=F32)
          for u, (bi, h) in enumerate(units)]

    m_prev = [m_ref[bi, h][:, 0:1] for bi, h in units]
    m_rows = [jnp.maximum(b_col[u] + m_prev[u], row_max[u]) for u in range(len(units))]
    inter = [jnp.exp(b_col[u] + m_prev[u] - m_rows[u]) for u in range(len(units))]
    w = [qk[u] * jnp.exp(dmat[u] - m_rows[u]) for u in range(len(units))]

    for u, (bi, h) in enumerate(units):
        v = v_ref[bi, :, v_cols[u]]
        num = inter[u] * qc[u] + jnp.dot(w[u].astype(BF16), v, preferred_element_type=F32)
        qn = jnp.sum(q[u].astype(F32) * n_ref[bi, h], axis=1, keepdims=True)
        den = inter[u] * qn + jnp.sum(w[u], axis=1, keepdims=True)
        hid = num * (1.0 / jnp.maximum(jnp.abs(den), jnp.exp(-m_rows[u])))
        og = og_ref[bi, :, v_cols[u]].astype(F32)
        o_ref[bi, :, v_cols[u]] = (hid * jax.nn.sigmoid(og)).astype(BF16)

    for u, (bi, h) in enumerate(units):
        v = v_ref[bi, :, v_cols[u]]
        m_end = m_rows[u][L - 1:L, :]
        b_end = b_col[u][L - 1:L, :]
        decay = jnp.exp(b_end + m_prev[u] - m_end)
        wk = jnp.exp(b_end - b_col[u] + ig_col[u] - m_end)
        kw = k[u].astype(F32) * wk
        c_ref[bi, h] = decay * c_ref[bi, h] + lax.dot_general(
            kw.astype(BF16), v, (((0,), (0,)), ((), ())), preferred_element_type=F32)
        n_ref[bi, h] = decay * n_ref[bi, h] + jnp.sum(kw, axis=0, keepdims=True)
        m_ref[bi, h] = jnp.broadcast_to(m_end, (1, LANES))


def _mlstm_core(proj, gates, batch, seq):
    nc = seq // CHUNK
    qk_cols = M_HEADS * M_QK_DIM
    v_cols = M_HEADS * M_V_DIM
    assert batch % M_BATCH == 0
    return pl.pallas_call(
        _mlstm_kernel,
        grid=(batch // M_BATCH, nc),
        in_specs=[
            pl.BlockSpec((M_BATCH, CHUNK, qk_cols), lambda b, c: (b, c, 0)),
            pl.BlockSpec((M_BATCH, CHUNK, qk_cols), lambda b, c: (b, c, 1)),
            pl.BlockSpec((M_BATCH, CHUNK, v_cols), lambda b, c: (b, c, 1)),
            pl.BlockSpec((M_BATCH, CHUNK, v_cols), lambda b, c: (b, c, 2)),
            pl.BlockSpec((M_BATCH, CHUNK, LANES), lambda b, c: (b, c, 0)),
        ],
        out_specs=pl.BlockSpec((M_BATCH, CHUNK, v_cols), lambda b, c: (b, c, 0)),
        out_shape=jax.ShapeDtypeStruct((batch, seq, v_cols), BF16),
        scratch_shapes=[
            pltpu.VMEM((M_BATCH, M_HEADS, M_QK_DIM, M_V_DIM), F32),
            pltpu.VMEM((M_BATCH, M_HEADS, 1, M_QK_DIM), F32),
            pltpu.VMEM((M_BATCH, M_HEADS, 1, LANES), F32),
        ],
        compiler_params=_compiler_params(("parallel", "arbitrary")),
        name="mlstm_core",
    )(proj, proj, proj, proj, gates)


def _mlstm_mixer(x, w_in, b_gates, w_o, g, b, batch, seq):
    w_main = w_in[:, :M_MAIN_COLS].astype(BF16)
    pad = LANES - M_GATE_COLS
    w_gates = jnp.pad(w_in[:, M_MAIN_COLS:], ((0, 0), (0, pad))).astype(BF16)
    bias = jnp.pad(b_gates.astype(F32), (0, pad)).reshape(1, LANES)
    proj, gates = _mlstm_proj(x, w_main, w_gates, bias)
    hid = _mlstm_core(proj.reshape(batch, seq, M_MAIN_COLS), gates.reshape(batch, seq, LANES),
                      batch, seq)
    return _outproj_ln(hid.reshape(batch * seq, D_MODEL), w_o.astype(BF16), x, g, b)


def kernel(x, ffn_w1, ffn_w3, ffn_w2, ln_g, ln_b, att_w_qkv, att_sinks, att_w_o,
           mlstm_w_in, mlstm_b_gates, mlstm_w_o):
    batch, seq, d = x.shape
    assert d == D_MODEL and seq % WINDOW == 0 and (batch * seq) % FFN_TM == 0
    xt = x.reshape(batch * seq, d)
    w1 = ffn_w1.astype(BF16)
    w3 = ffn_w3.astype(BF16)
    w2 = ffn_w2.astype(BF16)
    g = ln_g.astype(F32).reshape(DEPTH, 3, 1, d)
    b = ln_b.astype(F32).reshape(DEPTH, 3, 1, d)
    for layer in range(DEPTH):
        xt = _ffn_ln(xt, w1, w3, w2, g[layer, 0], b[layer, 0], layer, 0)
        j = layer // 2
        if layer % 2 == 0:
            xt = _attention_mixer(xt, att_w_qkv[j], att_sinks[j], att_w_o[j],
                                  g[layer, 1], b[layer, 1], batch, seq)
        else:
            xt = _mlstm_mixer(xt, mlstm_w_in[j], mlstm_b_gates[j], mlstm_w_o[j],
                              g[layer, 1], b[layer, 1], batch, seq)
        xt = _ffn_ln(xt, w1, w3, w2, g[layer, 2], b[layer, 2], layer, 1)
    return xt.reshape(batch, seq, d).astype(x.dtype)
```

```python
import functools

import jax
import jax.numpy as jnp
from jax import lax
from jax.experimental import pallas as pl
from jax.experimental.pallas import tpu as pltpu

D_MODEL = 2048
DEPTH = 4
D_FF = 5632
ATT_HEADS = 32
ATT_KV_HEADS = 4
ATT_GROUP = ATT_HEADS // ATT_KV_HEADS
ATT_HEAD_DIM = D_MODEL // ATT_HEADS
WINDOW = 128
M_HEADS = 4
M_V_DIM = D_MODEL // M_HEADS
M_QK_DIM = M_V_DIM // 2
CHUNK = 128
DEEPNORM_ALPHA = (2.0 * DEPTH) ** 0.25
LN_EPS = 1e-5

LANES = 128
MASKED = -1e30
NO_KEY_DISTANCE = 1e30
VMEM_LIMIT_BYTES = 56 * 1024 * 1024

BF16 = jnp.bfloat16
F32 = jnp.float32


def _compiler_params(semantics):
    return pltpu.CompilerParams(dimension_semantics=semantics,
                                vmem_limit_bytes=VMEM_LIMIT_BYTES)


def _layer_norm(y, g, b):
    mu = jnp.mean(y, axis=-1, keepdims=True)
    yc = y - mu
    var = jnp.mean(yc * yc, axis=-1, keepdims=True)
    return yc * lax.rsqrt(var + LN_EPS) * g + b


LN_ROWS = 32


def _layer_norm_rows(o_ref, first_row, n_rows, g, b, residual=None):
    for c in range(n_rows // LN_ROWS):
        rows = pl.ds(first_row + c * LN_ROWS, LN_ROWS)
        y = o_ref[rows, :]
        if residual is not None:
            y = y + residual(rows)
        o_ref[rows, :] = _layer_norm(y, g, b)


FFN_TM = 1024
FFN_SUB = 512
FFN_TF = 512
FFN_NF = D_FF // FFN_TF
FFN_PREFETCH_F = FFN_NF // 2
assert 0 < FFN_PREFETCH_F < FFN_NF


def _ffn_kernel(x_hbm, w1_ref, w3_ref, w2_ref, g_ref, b_ref, o_ref, xf_ref, xb_ref, sem):
    i = pl.program_id(0)
    f = pl.program_id(1)

    def x_copy(block):
        return pltpu.make_async_copy(x_hbm.at[pl.ds(block * FFN_TM, FFN_TM), :], xf_ref, sem)

    last = pl.num_programs(1) - 1

    def step(is_first, is_last):
        for r in range(FFN_TM // FFN_SUB):
            rows = pl.ds(r * FFN_SUB, FFN_SUB)
            if is_first:
                xf = xf_ref[rows, :]
                xb = xf.astype(BF16)
                xb_ref[rows, :] = xb
            else:
                xb = xb_ref[rows, :]
            h1 = jnp.dot(xb, w1_ref[...], preferred_element_type=F32)
            h3 = jnp.dot(xb, w3_ref[...], preferred_element_type=F32)
            act = (0.5 * (h1 * jax.nn.sigmoid(h1) * h3)).astype(BF16)
            part = jnp.dot(act, w2_ref[...], preferred_element_type=F32)
            o_ref[rows, :] = (DEEPNORM_ALPHA * xf if is_first else o_ref[rows, :]) + part
        if is_last:
            _layer_norm_rows(o_ref, 0, FFN_TM, g_ref[...], b_ref[...])

    @pl.when(f == 0)
    def _():
        @pl.when(i == 0)
        def _():
            x_copy(0).start()

        x_copy(i).wait()
        step(True, False)

    @pl.when(f == FFN_PREFETCH_F)
    def _():
        @pl.when(i + 1 < pl.num_programs(0))
        def _():
            x_copy(i + 1).start()

    @pl.when((f > 0) & (f < last))
    def _():
        step(False, False)

    @pl.when(f == last)
    def _():
        step(False, True)


def _ln_spec(layer, which):
    return pl.BlockSpec((None, None, 1, D_MODEL), lambda *_: (layer, which, 0, 0))


def _ffn_ln(x, w1, w3, w2, g, b, layer, half):
    t = x.shape[0]
    grid = (t // FFN_TM, FFN_NF)
    return pl.pallas_call(
        _ffn_kernel,
        grid=grid,
        in_specs=[
            pl.BlockSpec(memory_space=pl.ANY),
            pl.BlockSpec((None, None, D_MODEL, FFN_TF), lambda i, f: (layer, half, 0, f)),
            pl.BlockSpec((None, None, D_MODEL, FFN_TF), lambda i, f: (layer, half, 0, f)),
            pl.BlockSpec((None, None, FFN_TF, D_MODEL), lambda i, f: (layer, half, f, 0)),
            _ln_spec(layer, 2 * half),
            _ln_spec(layer, 2 * half),
        ],
        out_specs=pl.BlockSpec((FFN_TM, D_MODEL), lambda i, f: (i, 0)),
        out_shape=jax.ShapeDtypeStruct((t, D_MODEL), F32),
        scratch_shapes=[pltpu.VMEM((FFN_TM, D_MODEL), F32),
                        pltpu.VMEM((FFN_TM, D_MODEL), BF16),
                        pltpu.SemaphoreType.DMA(())],
        compiler_params=_compiler_params(("arbitrary", "arbitrary")),
        name="ffn_ln",
    )(x, w1, w3, w2, g, b)


OUT_TM = 512


OUT_SUB = 256


def _outproj_kernel(a_ref, w_ref, x_ref, g_ref, b_ref, o_ref):
    for r in range(OUT_TM // OUT_SUB):
        rows = pl.ds(r * OUT_SUB, OUT_SUB)
        o_ref[rows, :] = jnp.dot(a_ref[rows, :], w_ref[...], preferred_element_type=F32)
        _layer_norm_rows(o_ref, r * OUT_SUB, OUT_SUB, g_ref[...], b_ref[...],
                         residual=lambda ln_rows: DEEPNORM_ALPHA * x_ref[ln_rows, :])


def _outproj_ln(a, w, j, x, g, b, layer):
    t = x.shape[0]
    return pl.pallas_call(
        _outproj_kernel,
        grid=(t // OUT_TM,),
        in_specs=[
            pl.BlockSpec((OUT_TM, D_MODEL), lambda i: (i, 0)),
            pl.BlockSpec((None, D_MODEL, D_MODEL), lambda i: (j, 0, 0)),
            pl.BlockSpec((OUT_TM, D_MODEL), lambda i: (i, 0)),
            _ln_spec(layer, 1),
            _ln_spec(layer, 1),
        ],
        out_specs=pl.BlockSpec((OUT_TM, D_MODEL), lambda i: (i, 0)),
        out_shape=jax.ShapeDtypeStruct((t, D_MODEL), F32),
        compiler_params=_compiler_params(("parallel",)),
        name="outproj_ln",
    )(a, w, x, g, b)


QKV_TM = 512
ATT_KV_COLS = ATT_KV_HEADS * ATT_HEAD_DIM
ATT_PAIRS = ATT_HEADS // 2


def _qkv_kernel(x_ref, wq_ref, wk_ref, wv_ref, q_ref, k_ref, v_ref):
    xb = x_ref[...].astype(BF16)
    q_ref[...] = jnp.dot(xb, wq_ref[...], preferred_element_type=F32).astype(BF16)
    k_ref[...] = jnp.dot(xb, wk_ref[...], preferred_element_type=F32).astype(BF16)
    v_ref[...] = jnp.dot(xb, wv_ref[...], preferred_element_type=F32).astype(BF16)


def _qkv_proj(x, wq, wkv, j):
    t = x.shape[0]
    return pl.pallas_call(
        _qkv_kernel,
        grid=(t // QKV_TM,),
        in_specs=[
            pl.BlockSpec((QKV_TM, D_MODEL), lambda i: (i, 0)),
            pl.BlockSpec((None, D_MODEL, D_MODEL), lambda i: (j, 0, 0)),
            pl.BlockSpec((None, D_MODEL, ATT_KV_COLS), lambda i: (j, 0, 0)),
            pl.BlockSpec((None, D_MODEL, ATT_KV_COLS), lambda i: (j, 0, 1)),
        ],
        out_specs=[
            pl.BlockSpec((QKV_TM, D_MODEL), lambda i: (i, 0)),
            pl.BlockSpec((QKV_TM, ATT_KV_COLS), lambda i: (i, 0)),
            pl.BlockSpec((QKV_TM, ATT_KV_COLS), lambda i: (i, 0)),
        ],
        out_shape=[
            jax.ShapeDtypeStruct((t, D_MODEL), BF16),
            jax.ShapeDtypeStruct((t, ATT_KV_COLS), BF16),
            jax.ShapeDtypeStruct((t, ATT_KV_COLS), BF16),
        ],
        compiler_params=_compiler_params(("parallel",)),
        name="qkv_proj",
    )(x, wq, wkv, wkv)


def _alibi_slope(head):
    return 2.0 ** (-8.0 * (head + 1) / ATT_HEADS)


def _attn_kernel(sink_ref, q_ref, kc_ref, kp_ref, vc_ref, vp_ref, o_ref):
    n = pl.program_id(1)
    W = WINDOW
    rows = 2 * W
    qi = lax.broadcasted_iota(jnp.int32, (rows, W), 0) & (W - 1)
    j = lax.broadcasted_iota(jnp.int32, (rows, W), 1)
    from_prev = j > qi
    dist = jnp.where(from_prev, W + qi - j, qi - j).astype(F32)
    dist = jnp.where(from_prev & (n == 0), NO_KEY_DISTANCE, dist)
    is_lo_row = lax.broadcasted_iota(jnp.int32, (rows, 1), 0) < W
    lo_lane = lax.broadcasted_iota(jnp.int32, (1, LANES), 1) < ATT_HEAD_DIM
    zero = jnp.zeros((), BF16)

    for c2 in range(ATT_KV_HEADS // 2):
        cols = slice(LANES * c2, LANES * (c2 + 1))
        kk = jnp.concatenate([kp_ref[:, cols], kc_ref[:, cols]], axis=0)
        vv = jnp.concatenate([vp_ref[:, cols], vc_ref[:, cols]], axis=0)
        vcat = jnp.concatenate([jnp.where(lo_lane, vv, zero),
                                jnp.where(lo_lane, zero, vv)], axis=0)
        G = range(ATT_GROUP)
        pcols = [slice(LANES * (c2 * ATT_GROUP + g), LANES * (c2 * ATT_GROUP + g + 1)) for g in G]
        head_lo = [(2 * c2) * ATT_GROUP + g for g in G]
        head_hi = [(2 * c2 + 1) * ATT_GROUP + g for g in G]
        slope = [jnp.where(is_lo_row, _alibi_slope(head_lo[g]), _alibi_slope(head_hi[g])) for g in G]
        sink = [jnp.where(is_lo_row, sink_ref[head_lo[g]], sink_ref[head_hi[g]]) for g in G]
        qp = [q_ref[:, pcols[g]] * jnp.asarray(ATT_HEAD_DIM ** -0.5, BF16) for g in G]
        qs = [jnp.concatenate([jnp.where(lo_lane, qp[g], zero),
                               jnp.where(lo_lane, zero, qp[g])], axis=0) for g in G]
        s2 = [lax.dot_general(qs[g], kk, (((1,), (1,)), ((), ())),
                              preferred_element_type=F32) for g in G]
        s = [jnp.where(from_prev, s2[g][:, :W], s2[g][:, W:]) - slope[g] * dist for g in G]
        m = [jnp.maximum(jnp.max(s[g], axis=-1, keepdims=True), sink[g]) for g in G]
        p = [jnp.exp(s[g] - m[g]) for g in G]
        denom = [jnp.sum(p[g], axis=-1, keepdims=True) + jnp.exp(sink[g] - m[g]) for g in G]
        p = [p[g] * (1.0 / denom[g]) for g in G]
        for g in G:
            p_prev = jnp.where(from_prev, p[g], 0.0).astype(BF16)
            p_cur = jnp.where(from_prev, 0.0, p[g]).astype(BF16)
            pcat = jnp.concatenate([p_prev[:W], p_cur[:W], p_prev[W:], p_cur[W:]], axis=1)
            o = jnp.dot(pcat, vcat, preferred_element_type=F32)
            o_ref[:, pcols[g]] = o.astype(BF16)


def _attn_core(sinks, q, k, v, batch, seq):
    nb = seq // WINDOW
    cur = lambda b, n: (b * nb + n, 0)
    prev = lambda b, n: (b * nb + jnp.maximum(n - 1, 0), 0)
    return pl.pallas_call(
        _attn_kernel,
        grid=(batch, nb),
        in_specs=[
            pl.BlockSpec(memory_space=pltpu.SMEM),
            pl.BlockSpec((WINDOW, D_MODEL), cur),
            pl.BlockSpec((WINDOW, ATT_KV_COLS), cur),
            pl.BlockSpec((WINDOW, ATT_KV_COLS), prev),
            pl.BlockSpec((WINDOW, ATT_KV_COLS), cur),
            pl.BlockSpec((WINDOW, ATT_KV_COLS), prev),
        ],
        out_specs=pl.BlockSpec((WINDOW, D_MODEL), cur),
        out_shape=jax.ShapeDtypeStruct((batch * seq, D_MODEL), BF16),
        compiler_params=_compiler_params(("parallel", "arbitrary")),
        name="swa_core",
    )(sinks, q, k, k, v, v)


def _to_head_pair_order(w, axis):
    shape = w.shape
    split = shape[:axis] + (ATT_KV_HEADS // 2, 2, ATT_GROUP, ATT_HEAD_DIM) + shape[axis + 1:]
    order = list(range(len(split)))
    order[axis + 1], order[axis + 2] = order[axis + 2], order[axis + 1]
    return w.reshape(split).transpose(order).reshape(shape)


def _attention_weights(att_w_qkv, att_w_o):
    nq = ATT_HEADS * ATT_HEAD_DIM
    wq = _to_head_pair_order(att_w_qkv[:, :, :nq], 2).astype(BF16)
    wkv = att_w_qkv[:, :, nq:].astype(BF16)
    wo = _to_head_pair_order(att_w_o, 1).astype(BF16)
    return wq, wkv, wo


def _attention_mixer(x, wq, wkv, wo, sinks, j, g, b, layer, batch, seq):
    q, k, v = _qkv_proj(x, wq, wkv, j)
    o = _attn_core(sinks, q, k, v, batch, seq)
    return _outproj_ln(o, wo, j, x, g, b, layer)


MPROJ_TM = 1024
MPROJ_TN = 1536
M_MAIN_COLS = 2 * M_HEADS * M_QK_DIM + 2 * M_HEADS * M_V_DIM
M_GATE_COLS = 2 * M_HEADS


def _mproj_kernel(x_ref, w_ref, wg_ref, bg_ref, p_ref, gates_ref, xb_ref):
    j = pl.program_id(1)

    @pl.when(j == 0)
    def _():
        xb = x_ref[...].astype(BF16)
        xb_ref[...] = xb
        p_ref[...] = jnp.dot(xb, w_ref[...], preferred_element_type=F32).astype(BF16)
        gates_ref[...] = jnp.dot(xb, wg_ref[...], preferred_element_type=F32) + bg_ref[...]

    @pl.when(j > 0)
    def _():
        p_ref[...] = jnp.dot(xb_ref[...], w_ref[...], preferred_element_type=F32).astype(BF16)


def _mlstm_proj(x, w_in, w_gates, b_gates, layer_j):
    t = x.shape[0]
    return pl.pallas_call(
        _mproj_kernel,
        grid=(t // MPROJ_TM, M_MAIN_COLS // MPROJ_TN),
        in_specs=[
            pl.BlockSpec((MPROJ_TM, D_MODEL), lambda i, j: (i, 0)),
            pl.BlockSpec((None, D_MODEL, MPROJ_TN), lambda i, j: (layer_j, 0, j)),
            pl.BlockSpec((None, D_MODEL, LANES), lambda i, j: (layer_j, 0, 0)),
            pl.BlockSpec((None, 1, LANES), lambda i, j: (layer_j, 0, 0)),
        ],
        out_specs=[
            pl.BlockSpec((MPROJ_TM, MPROJ_TN), lambda i, j: (i, j)),
            pl.BlockSpec((MPROJ_TM, LANES), lambda i, j: (i, 0)),
        ],
        out_shape=[
            jax.ShapeDtypeStruct((t, M_MAIN_COLS), BF16),
            jax.ShapeDtypeStruct((t, LANES), F32),
        ],
        scratch_shapes=[pltpu.VMEM((MPROJ_TM, D_MODEL), BF16)],
        compiler_params=_compiler_params(("parallel", "arbitrary")),
        name="mlstm_proj",
    )(x, w_in, w_gates, b_gates)


def _log_sigmoid(x):
    return jnp.minimum(x, 0.0) - jnp.log1p(jnp.exp(-jnp.abs(x)))


M_BATCH = 2


def _mlstm_kernel(q_ref, k_ref, v_ref, og_ref, gates_ref, o_ref, c_ref, n_ref, m_ref):
    L = CHUNK

    @pl.when(pl.program_id(1) == 0)
    def _():
        c_ref[...] = jnp.zeros_like(c_ref)
        n_ref[...] = jnp.zeros_like(n_ref)
        m_ref[...] = jnp.zeros_like(m_ref)

    t_idx = lax.broadcasted_iota(jnp.int32, (L, L), 0)
    s_idx = lax.broadcasted_iota(jnp.int32, (L, L), 1)
    causal = s_idx <= t_idx

    units = [(bi, h) for bi in range(M_BATCH) for h in range(M_HEADS)]
    qk_cols = [slice(h * M_QK_DIM, (h + 1) * M_QK_DIM) for _, h in units]
    v_cols = [slice(h * M_V_DIM, (h + 1) * M_V_DIM) for _, h in units]
    gates = [gates_ref[bi] for bi in range(M_BATCH)]
    gates_t = [gt.T for gt in gates]

    ig_col, b_col, dmat, row_max = [], [], [], []
    for bi, h in units:
        ig_col.append(gates[bi][:, h:h + 1])
        ig_row = gates_t[bi][h:h + 1, :]
        lf_col = _log_sigmoid(gates[bi][:, M_HEADS + h:M_HEADS + h + 1])
        lf_row = _log_sigmoid(gates_t[bi][M_HEADS + h:M_HEADS + h + 1, :])
        bc = jnp.sum(jnp.where(causal, lf_row, 0.0), axis=1, keepdims=True)
        br = jnp.sum(jnp.where(t_idx <= s_idx, lf_col, 0.0), axis=0, keepdims=True)
        d = jnp.where(causal, bc - br + ig_row, MASKED)
        b_col.append(bc)
        dmat.append(d)
        row_max.append(jnp.max(d, axis=1, keepdims=True))

    q = [q_ref[bi, :, qk_cols[u]] for u, (bi, _) in enumerate(units)]
    k = [k_ref[bi, :, qk_cols[u]] * jnp.asarray(M_QK_DIM ** -0.5, BF16)
         for u, (bi, _) in enumerate(units)]
    qk = [lax.dot_general(q[u], k[u], (((1,), (1,)), ((), ())), preferred_element_type=F32)
          for u in range(len(units))]
    qc = [jnp.dot(q[u], c_ref[bi, h].astype(BF16), preferred_element_type=F32)
          for u, (bi, h) in enumerate(units)]

    m_prev = [m_ref[bi, h][:, 0:1] for bi, h in units]
    m_rows = [jnp.maximum(b_col[u] + m_prev[u], row_max[u]) for u in range(len(units))]
    inter = [jnp.exp(b_col[u] + m_prev[u] - m_rows[u]) for u in range(len(units))]
    w = [qk[u] * jnp.exp(dmat[u] - m_rows[u]) for u in range(len(units))]

    for u, (bi, h) in enumerate(units):
        v = v_ref[bi, :, v_cols[u]]
        num = inter[u] * qc[u] + jnp.dot(w[u].astype(BF16), v, preferred_element_type=F32)
        qn = jnp.sum(q[u].astype(F32) * n_ref[bi, h], axis=1, keepdims=True)
        den = inter[u] * qn + jnp.sum(w[u], axis=1, keepdims=True)
        hid = num * (1.0 / jnp.maximum(jnp.abs(den), jnp.exp(-m_rows[u])))
        og = og_ref[bi, :, v_cols[u]].astype(F32)
        o_ref[bi, :, v_cols[u]] = (hid * jax.nn.sigmoid(og)).astype(BF16)

    for u, (bi, h) in enumerate(units):
        v = v_ref[bi, :, v_cols[u]]
        m_end = m_rows[u][L - 1:L, :]
        b_end = b_col[u][L - 1:L, :]
        decay = jnp.exp(b_end + m_prev[u] - m_end)
        wk = jnp.exp(b_end - b_col[u] + ig_col[u] - m_end)
        kw = k[u].astype(F32) * wk
        c_ref[bi, h] = decay * c_ref[bi, h] + lax.dot_general(
            kw.astype(BF16), v, (((0,), (0,)), ((), ())), preferred_element_type=F32)
        n_ref[bi, h] = decay * n_ref[bi, h] + jnp.sum(kw, axis=0, keepdims=True)
        m_ref[bi, h] = jnp.broadcast_to(m_end, (1, LANES))


def _mlstm_core(proj, gates, batch, seq):
    nc = seq // CHUNK
    qk_cols = M_HEADS * M_QK_DIM
    v_cols = M_HEADS * M_V_DIM
    assert batch % M_BATCH == 0
    return pl.pallas_call(
        _mlstm_kernel,
        grid=(batch // M_BATCH, nc),
        in_specs=[
            pl.BlockSpec((M_BATCH, CHUNK, qk_cols), lambda b, c: (b, c, 0)),
            pl.BlockSpec((M_BATCH, CHUNK, qk_cols), lambda b, c: (b, c, 1)),
            pl.BlockSpec((M_BATCH, CHUNK, v_cols), lambda b, c: (b, c, 1)),
            pl.BlockSpec((M_BATCH, CHUNK, v_cols), lambda b, c: (b, c, 2)),
            pl.BlockSpec((M_BATCH, CHUNK, LANES), lambda b, c: (b, c, 0)),
        ],
        out_specs=pl.BlockSpec((M_BATCH, CHUNK, v_cols), lambda b, c: (b, c, 0)),
        out_shape=jax.ShapeDtypeStruct((batch, seq, v_cols), BF16),
        scratch_shapes=[
            pltpu.VMEM((M_BATCH, M_HEADS, M_QK_DIM, M_V_DIM), F32),
            pltpu.VMEM((M_BATCH, M_HEADS, 1, M_QK_DIM), F32),
            pltpu.VMEM((M_BATCH, M_HEADS, 1, LANES), F32),
        ],
        compiler_params=_compiler_params(("parallel", "arbitrary")),
        name="mlstm_core",
    )(proj, proj, proj, proj, gates)


def _mlstm_weights(mlstm_w_in, mlstm_b_gates, mlstm_w_o):
    pad = LANES - M_GATE_COLS
    w_in = mlstm_w_in.astype(BF16)
    w_gates = jnp.pad(mlstm_w_in[:, :, M_MAIN_COLS:], ((0, 0), (0, 0), (0, pad))).astype(BF16)
    bias = jnp.pad(mlstm_b_gates.astype(F32), ((0, 0), (0, pad)))[:, None, :]
    return w_in, w_gates, bias, mlstm_w_o.astype(BF16)


def _mlstm_mixer(x, w_in, w_gates, bias, w_o, j, g, b, layer, batch, seq):
    proj, gates = _mlstm_proj(x, w_in, w_gates, bias, j)
    hid = _mlstm_core(proj.reshape(batch, seq, M_MAIN_COLS), gates.reshape(batch, seq, LANES),
                      batch, seq)
    return _outproj_ln(hid.reshape(batch * seq, D_MODEL), w_o, j, x, g, b, layer)


def kernel(x, ffn_w1, ffn_w3, ffn_w2, ln_g, ln_b, att_w_qkv, att_sinks, att_w_o,
           mlstm_w_in, mlstm_b_gates, mlstm_w_o):
    batch, seq, d = x.shape
    assert d == D_MODEL and seq % WINDOW == 0 and (batch * seq) % FFN_TM == 0
    xt = x.reshape(batch * seq, d)
    w1 = ffn_w1.astype(BF16)
    w3 = ffn_w3.astype(BF16)
    w2 = ffn_w2.astype(BF16)
    g = ln_g.astype(F32).reshape(DEPTH, 3, 1, d)
    b = ln_b.astype(F32).reshape(DEPTH, 3, 1, d)
    wq, wkv, att_wo = _attention_weights(att_w_qkv, att_w_o)
    m_w_in, m_w_gates, m_bias, m_wo = _mlstm_weights(mlstm_w_in, mlstm_b_gates, mlstm_w_o)
    sinks = att_sinks.astype(F32)
    for layer in range(DEPTH):
        xt = _ffn_ln(xt, w1, w3, w2, g, b, layer, 0)
        j = layer // 2
        if layer % 2 == 0:
            xt = _attention_mixer(xt, wq, wkv, att_wo, sinks[j], j, g, b, layer, batch, seq)
        else:
            xt = _mlstm_mixer(xt, m_w_in, m_w_gates, m_bias, m_wo, j, g, b, layer, batch, seq)
        xt = _ffn_ln(xt, w1, w3, w2, g, b, layer, 1)
    return xt.reshape(batch, seq, d).astype(x.dtype)
```

```python
import functools

import jax
import jax.numpy as jnp
from jax import lax
from jax.experimental import pallas as pl
from jax.experimental.pallas import tpu as pltpu

D_MODEL = 2048
DEPTH = 4
D_FF = 5632
ATT_HEADS = 32
ATT_KV_HEADS = 4
ATT_GROUP = ATT_HEADS // ATT_KV_HEADS
ATT_HEAD_DIM = D_MODEL // ATT_HEADS
WINDOW = 128
M_HEADS = 4
M_V_DIM = D_MODEL // M_HEADS
M_QK_DIM = M_V_DIM // 2
CHUNK = 128
DEEPNORM_ALPHA = (2.0 * DEPTH) ** 0.25
LN_EPS = 1e-5

LANES = 128
MASKED = -1e30
NO_KEY_DISTANCE = 1e30
VMEM_LIMIT_BYTES = 56 * 1024 * 1024

BF16 = jnp.bfloat16
F32 = jnp.float32


def _compiler_params(semantics):
    return pltpu.CompilerParams(dimension_semantics=semantics,
                                vmem_limit_bytes=VMEM_LIMIT_BYTES)


def _layer_norm(y, g, b):
    mu = jnp.mean(y, axis=-1, keepdims=True)
    yc = y - mu
    var = jnp.mean(yc * yc, axis=-1, keepdims=True)
    return yc * lax.rsqrt(var + LN_EPS) * g + b


LN_ROWS = 16


def _layer_norm_rows(o_ref, first_row, n_rows, g, b, residual=None):
    for c in range(n_rows // LN_ROWS):
        rows = pl.ds(first_row + c * LN_ROWS, LN_ROWS)
        y = o_ref[rows, :]
        if residual is not None:
            y = y + residual(rows)
        o_ref[rows, :] = _layer_norm(y, g, b)


FFN_TM = 1024
FFN_SUB = 512
FFN_TF = 512
FFN_NF = D_FF // FFN_TF
FFN_PREFETCH_F = FFN_NF // 2
assert 0 < FFN_PREFETCH_F < FFN_NF


def _ffn_kernel(x_hbm, w1_ref, w3_ref, w2_ref, g_ref, b_ref, o_ref, xf_ref, xb_ref, sem):
    i = pl.program_id(0)
    f = pl.program_id(1)

    def x_copy(block):
        return pltpu.make_async_copy(x_hbm.at[pl.ds(block * FFN_TM, FFN_TM), :], xf_ref, sem)

    last = pl.num_programs(1) - 1

    def step(is_first, is_last):
        for r in range(FFN_TM // FFN_SUB):
            rows = pl.ds(r * FFN_SUB, FFN_SUB)
            if is_first:
                xf = xf_ref[rows, :]
                xb = xf.astype(BF16)
                xb_ref[rows, :] = xb
            else:
                xb = xb_ref[rows, :]
            h1 = jnp.dot(xb, w1_ref[...], preferred_element_type=F32)
            h3 = jnp.dot(xb, w3_ref[...], preferred_element_type=F32)
            act = (0.5 * (h1 * jax.nn.sigmoid(h1) * h3)).astype(BF16)
            part = jnp.dot(act, w2_ref[...], preferred_element_type=F32)
            if is_last and r > 0:
                _layer_norm_rows(o_ref, (r - 1) * FFN_SUB, FFN_SUB, g_ref[...], b_ref[...])
            o_ref[rows, :] = (DEEPNORM_ALPHA * xf if is_first else o_ref[rows, :]) + part
        if is_last:
            _layer_norm_rows(o_ref, FFN_TM - FFN_SUB, FFN_SUB, g_ref[...], b_ref[...])

    @pl.when(f == 0)
    def _():
        @pl.when(i == 0)
        def _():
            x_copy(0).start()

        x_copy(i).wait()
        step(True, False)

    @pl.when(f == FFN_PREFETCH_F)
    def _():
        @pl.when(i + 1 < pl.num_programs(0))
        def _():
            x_copy(i + 1).start()

    @pl.when((f > 0) & (f < last))
    def _():
        step(False, False)

    @pl.when(f == last)
    def _():
        step(False, True)


def _column_tile_major(w, tile):
    *lead, k, n = w.shape
    nl = len(lead)
    w = w.reshape(*lead, k, n // tile, tile)
    return w.transpose(*range(nl), nl + 1, nl, nl + 2)


def _ln_spec(layer, which):
    return pl.BlockSpec((None, None, 1, D_MODEL), lambda *_: (layer, which, 0, 0))


def _ffn_ln(x, w1, w3, w2, g, b, layer, half):
    t = x.shape[0]
    grid = (t // FFN_TM, FFN_NF)
    return pl.pallas_call(
        _ffn_kernel,
        grid=grid,
        in_specs=[
            pl.BlockSpec(memory_space=pl.ANY),
            pl.BlockSpec((None, None, None, D_MODEL, FFN_TF), lambda i, f: (layer, half, f, 0, 0)),
            pl.BlockSpec((None, None, None, D_MODEL, FFN_TF), lambda i, f: (layer, half, f, 0, 0)),
            pl.BlockSpec((None, None, FFN_TF, D_MODEL), lambda i, f: (layer, half, f, 0)),
            _ln_spec(layer, 2 * half),
            _ln_spec(layer, 2 * half),
        ],
        out_specs=pl.BlockSpec((FFN_TM, D_MODEL), lambda i, f: (i, 0)),
        out_shape=jax.ShapeDtypeStruct((t, D_MODEL), F32),
        scratch_shapes=[pltpu.VMEM((FFN_TM, D_MODEL), F32),
                        pltpu.VMEM((FFN_TM, D_MODEL), BF16),
                        pltpu.SemaphoreType.DMA(())],
        compiler_params=_compiler_params(("arbitrary", "arbitrary")),
        name="ffn_ln",
    )(x, w1, w3, w2, g, b)


OUT_TM = 512


OUT_SUB = 256


def _outproj_kernel(a_ref, w_ref, x_ref, g_ref, b_ref, o_ref):
    for r in range(OUT_TM // OUT_SUB):
        rows = pl.ds(r * OUT_SUB, OUT_SUB)
        o_ref[rows, :] = jnp.dot(a_ref[rows, :], w_ref[...], preferred_element_type=F32)
        _layer_norm_rows(o_ref, r * OUT_SUB, OUT_SUB, g_ref[...], b_ref[...],
                         residual=lambda ln_rows: DEEPNORM_ALPHA * x_ref[ln_rows, :])


def _outproj_ln(a, w, j, x, g, b, layer):
    t = x.shape[0]
    return pl.pallas_call(
        _outproj_kernel,
        grid=(t // OUT_TM,),
        in_specs=[
            pl.BlockSpec((OUT_TM, D_MODEL), lambda i: (i, 0)),
            pl.BlockSpec((None, D_MODEL, D_MODEL), lambda i: (j, 0, 0)),
            pl.BlockSpec((OUT_TM, D_MODEL), lambda i: (i, 0)),
            _ln_spec(layer, 1),
            _ln_spec(layer, 1),
        ],
        out_specs=pl.BlockSpec((OUT_TM, D_MODEL), lambda i: (i, 0)),
        out_shape=jax.ShapeDtypeStruct((t, D_MODEL), F32),
        compiler_params=_compiler_params(("parallel",)),
        name="outproj_ln",
    )(a, w, x, g, b)


QKV_TM = 512
ATT_KV_COLS = ATT_KV_HEADS * ATT_HEAD_DIM
ATT_PAIRS = ATT_HEADS // 2


def _qkv_kernel(x_ref, wq_ref, wk_ref, wv_ref, q_ref, k_ref, v_ref):
    xb = x_ref[...].astype(BF16)
    q_ref[...] = jnp.dot(xb, wq_ref[...], preferred_element_type=F32).astype(BF16)
    k_ref[...] = jnp.dot(xb, wk_ref[...], preferred_element_type=F32).astype(BF16)
    v_ref[...] = jnp.dot(xb, wv_ref[...], preferred_element_type=F32).astype(BF16)


def _qkv_proj(x, wq, wkv, j):
    t = x.shape[0]
    return pl.pallas_call(
        _qkv_kernel,
        grid=(t // QKV_TM,),
        in_specs=[
            pl.BlockSpec((QKV_TM, D_MODEL), lambda i: (i, 0)),
            pl.BlockSpec((None, D_MODEL, D_MODEL), lambda i: (j, 0, 0)),
            pl.BlockSpec((None, D_MODEL, ATT_KV_COLS), lambda i: (j, 0, 0)),
            pl.BlockSpec((None, D_MODEL, ATT_KV_COLS), lambda i: (j, 0, 1)),
        ],
        out_specs=[
            pl.BlockSpec((QKV_TM, D_MODEL), lambda i: (i, 0)),
            pl.BlockSpec((QKV_TM, ATT_KV_COLS), lambda i: (i, 0)),
            pl.BlockSpec((QKV_TM, ATT_KV_COLS), lambda i: (i, 0)),
        ],
        out_shape=[
            jax.ShapeDtypeStruct((t, D_MODEL), BF16),
            jax.ShapeDtypeStruct((t, ATT_KV_COLS), BF16),
            jax.ShapeDtypeStruct((t, ATT_KV_COLS), BF16),
        ],
        compiler_params=_compiler_params(("parallel",)),
        name="qkv_proj",
    )(x, wq, wkv, wkv)


def _alibi_slope(head):
    return 2.0 ** (-8.0 * (head + 1) / ATT_HEADS)


def _attn_kernel(sink_ref, q_ref, kc_ref, kp_ref, vc_ref, vp_ref, o_ref):
    n = pl.program_id(1)
    W = WINDOW
    rows = 2 * W
    qi = lax.broadcasted_iota(jnp.int32, (rows, W), 0) & (W - 1)
    j = lax.broadcasted_iota(jnp.int32, (rows, W), 1)
    from_prev = j > qi
    dist = jnp.where(from_prev, W + qi - j, qi - j).astype(F32)
    dist = jnp.where(from_prev & (n == 0), NO_KEY_DISTANCE, dist)
    is_lo_row = lax.broadcasted_iota(jnp.int32, (rows, 1), 0) < W
    lo_lane = lax.broadcasted_iota(jnp.int32, (1, LANES), 1) < ATT_HEAD_DIM
    zero = jnp.zeros((), BF16)

    for c2 in range(ATT_KV_HEADS // 2):
        cols = slice(LANES * c2, LANES * (c2 + 1))
        kk = jnp.concatenate([kp_ref[:, cols], kc_ref[:, cols]], axis=0)
        vv = jnp.concatenate([vp_ref[:, cols], vc_ref[:, cols]], axis=0)
        vcat = jnp.concatenate([jnp.where(lo_lane, vv, zero),
                                jnp.where(lo_lane, zero, vv)], axis=0)
        G = range(ATT_GROUP)
        pcols = [slice(LANES * (c2 * ATT_GROUP + g), LANES * (c2 * ATT_GROUP + g + 1)) for g in G]
        head_lo = [(2 * c2) * ATT_GROUP + g for g in G]
        head_hi = [(2 * c2 + 1) * ATT_GROUP + g for g in G]
        slope = [jnp.where(is_lo_row, _alibi_slope(head_lo[g]), _alibi_slope(head_hi[g])) for g in G]
        sink = [jnp.where(is_lo_row, sink_ref[head_lo[g]], sink_ref[head_hi[g]]) for g in G]
        qp = [q_ref[:, pcols[g]] * jnp.asarray(ATT_HEAD_DIM ** -0.5, BF16) for g in G]
        qs = [jnp.concatenate([jnp.where(lo_lane, qp[g], zero),
                               jnp.where(lo_lane, zero, qp[g])], axis=0) for g in G]
        s2 = [lax.dot_general(qs[g], kk, (((1,), (1,)), ((), ())),
                              preferred_element_type=F32) for g in G]
        s = [jnp.where(from_prev, s2[g][:, :W], s2[g][:, W:]) - slope[g] * dist for g in G]
        m = [jnp.maximum(jnp.max(s[g], axis=-1, keepdims=True), sink[g]) for g in G]
        p = [jnp.exp(s[g] - m[g]) for g in G]
        denom = [jnp.sum(p[g], axis=-1, keepdims=True) + jnp.exp(sink[g] - m[g]) for g in G]
        p = [p[g] * (1.0 / denom[g]) for g in G]
        for g in G:
            p_prev = jnp.where(from_prev, p[g], 0.0).astype(BF16)
            p_cur = jnp.where(from_prev, 0.0, p[g]).astype(BF16)
            pcat = jnp.concatenate([p_prev[:W], p_cur[:W], p_prev[W:], p_cur[W:]], axis=1)
            o = jnp.dot(pcat, vcat, preferred_element_type=F32)
            o_ref[:, pcols[g]] = o.astype(BF16)


def _attn_core(sinks, q, k, v, batch, seq):
    nb = seq // WINDOW
    cur = lambda b, n: (b * nb + n, 0)
    prev = lambda b, n: (b * nb + jnp.maximum(n - 1, 0), 0)
    return pl.pallas_call(
        _attn_kernel,
        grid=(batch, nb),
        in_specs=[
            pl.BlockSpec(memory_space=pltpu.SMEM),
            pl.BlockSpec((WINDOW, D_MODEL), cur),
            pl.BlockSpec((WINDOW, ATT_KV_COLS), cur),
            pl.BlockSpec((WINDOW, ATT_KV_COLS), prev),
            pl.BlockSpec((WINDOW, ATT_KV_COLS), cur),
            pl.BlockSpec((WINDOW, ATT_KV_COLS), prev),
        ],
        out_specs=pl.BlockSpec((WINDOW, D_MODEL), cur),
        out_shape=jax.ShapeDtypeStruct((batch * seq, D_MODEL), BF16),
        compiler_params=_compiler_params(("parallel", "arbitrary")),
        name="swa_core",
    )(sinks, q, k, k, v, v)


def _to_head_pair_order(w, axis):
    shape = w.shape
    split = shape[:axis] + (ATT_KV_HEADS // 2, 2, ATT_GROUP, ATT_HEAD_DIM) + shape[axis + 1:]
    order = list(range(len(split)))
    order[axis + 1], order[axis + 2] = order[axis + 2], order[axis + 1]
    return w.reshape(split).transpose(order).reshape(shape)


def _attention_weights(att_w_qkv, att_w_o):
    nq = ATT_HEADS * ATT_HEAD_DIM
    wq = _to_head_pair_order(att_w_qkv[:, :, :nq], 2).astype(BF16)
    wkv = att_w_qkv[:, :, nq:].astype(BF16)
    wo = _to_head_pair_order(att_w_o, 1).astype(BF16)
    return wq, wkv, wo


def _attention_mixer(x, wq, wkv, wo, sinks, j, g, b, layer, batch, seq):
    q, k, v = _qkv_proj(x, wq, wkv, j)
    o = _attn_core(sinks, q, k, v, batch, seq)
    return _outproj_ln(o, wo, j, x, g, b, layer)


MPROJ_TM = 1024
MPROJ_TN = 1536
M_MAIN_COLS = 2 * M_HEADS * M_QK_DIM + 2 * M_HEADS * M_V_DIM
M_GATE_COLS = 2 * M_HEADS


def _mproj_kernel(x_ref, w_ref, wg_ref, bg_ref, p_ref, gates_ref, xb_ref):
    j = pl.program_id(1)

    @pl.when(j == 0)
    def _():
        xb = x_ref[...].astype(BF16)
        xb_ref[...] = xb
        p_ref[...] = jnp.dot(xb, w_ref[...], preferred_element_type=F32).astype(BF16)
        gates_ref[...] = jnp.dot(xb, wg_ref[...], preferred_element_type=F32) + bg_ref[...]

    @pl.when(j > 0)
    def _():
        p_ref[...] = jnp.dot(xb_ref[...], w_ref[...], preferred_element_type=F32).astype(BF16)


def _mlstm_proj(x, w_in, w_gates, b_gates, layer_j):
    t = x.shape[0]
    return pl.pallas_call(
        _mproj_kernel,
        grid=(t // MPROJ_TM, M_MAIN_COLS // MPROJ_TN),
        in_specs=[
            pl.BlockSpec((MPROJ_TM, D_MODEL), lambda i, j: (i, 0)),
            pl.BlockSpec((None, None, D_MODEL, MPROJ_TN), lambda i, j: (layer_j, j, 0, 0)),
            pl.BlockSpec((None, D_MODEL, LANES), lambda i, j: (layer_j, 0, 0)),
            pl.BlockSpec((None, 1, LANES), lambda i, j: (layer_j, 0, 0)),
        ],
        out_specs=[
            pl.BlockSpec((MPROJ_TM, MPROJ_TN), lambda i, j: (i, j)),
            pl.BlockSpec((MPROJ_TM, LANES), lambda i, j: (i, 0)),
        ],
        out_shape=[
            jax.ShapeDtypeStruct((t, M_MAIN_COLS), BF16),
            jax.ShapeDtypeStruct((t, LANES), F32),
        ],
        scratch_shapes=[pltpu.VMEM((MPROJ_TM, D_MODEL), BF16)],
        compiler_params=_compiler_params(("parallel", "arbitrary")),
        name="mlstm_proj",
    )(x, w_in, w_gates, b_gates)


def _log_sigmoid(x):
    return jnp.minimum(x, 0.0) - jnp.log1p(jnp.exp(-jnp.abs(x)))


M_BATCH = 2


def _mlstm_kernel(q_ref, k_ref, v_ref, og_ref, gates_ref, o_ref, c_ref, n_ref, m_ref):
    L = CHUNK

    @pl.when(pl.program_id(1) == 0)
    def _():
        c_ref[...] = jnp.zeros_like(c_ref)
        n_ref[...] = jnp.zeros_like(n_ref)
        m_ref[...] = jnp.zeros_like(m_ref)

    t_idx = lax.broadcasted_iota(jnp.int32, (L, L), 0)
    s_idx = lax.broadcasted_iota(jnp.int32, (L, L), 1)
    causal = s_idx <= t_idx

    units = [(bi, h) for bi in range(M_BATCH) for h in range(M_HEADS)]
    qk_cols = [slice(h * M_QK_DIM, (h + 1) * M_QK_DIM) for _, h in units]
    v_cols = [slice(h * M_V_DIM, (h + 1) * M_V_DIM) for _, h in units]
    gates = [gates_ref[bi] for bi in range(M_BATCH)]
    gates_t = [gt.T for gt in gates]

    ig_col, b_col, dmat, row_max = [], [], [], []
    for bi, h in units:
        ig_col.append(gates[bi][:, h:h + 1])
        ig_row = gates_t[bi][h:h + 1, :]
        lf_col = _log_sigmoid(gates[bi][:, M_HEADS + h:M_HEADS + h + 1])
        lf_row = _log_sigmoid(gates_t[bi][M_HEADS + h:M_HEADS + h + 1, :])
        bc = jnp.sum(jnp.where(causal, lf_row, 0.0), axis=1, keepdims=True)
        br = jnp.sum(jnp.where(t_idx <= s_idx, lf_col, 0.0), axis=0, keepdims=True)
        d = jnp.where(causal, bc - br + ig_row, MASKED)
        b_col.append(bc)
        dmat.append(d)
        row_max.append(jnp.max(d, axis=1, keepdims=True))

    q = [q_ref[bi, :, qk_cols[u]] for u, (bi, _) in enumerate(units)]
    k = [k_ref[bi, :, qk_cols[u]] * jnp.asarray(M_QK_DIM ** -0.5, BF16)
         for u, (bi, _) in enumerate(units)]
    qk = [lax.dot_general(q[u], k[u], (((1,), (1,)), ((), ())), preferred_element_type=F32)
          for u in range(len(units))]
    qc = [jnp.dot(q[u], c_ref[bi, h].astype(BF16), preferred_element_type=F32)
          for u, (bi, h) in enumerate(units)]

    m_prev = [m_ref[bi, h][:, 0:1] for bi, h in units]
    m_rows = [jnp.maximum(b_col[u] + m_prev[u], row_max[u]) for u in range(len(units))]
    inter = [jnp.exp(b_col[u] + m_prev[u] - m_rows[u]) for u in range(len(units))]
    w = [qk[u] * jnp.exp(dmat[u] - m_rows[u]) for u in range(len(units))]

    for u, (bi, h) in enumerate(units):
        v = v_ref[bi, :, v_cols[u]]
        num = inter[u] * qc[u] + jnp.dot(w[u].astype(BF16), v, preferred_element_type=F32)
        qn = jnp.sum(q[u].astype(F32) * n_ref[bi, h], axis=1, keepdims=True)
        den = inter[u] * qn + jnp.sum(w[u], axis=1, keepdims=True)
        hid = num * (1.0 / jnp.maximum(jnp.abs(den), jnp.exp(-m_rows[u])))
        og = og_ref[bi, :, v_cols[u]].astype(F32)
        o_ref[bi, :, v_cols[u]] = (hid * jax.nn.sigmoid(og)).astype(BF16)

    for u, (bi, h) in enumerate(units):
        v = v_ref[bi, :, v_cols[u]]
        m_end = m_rows[u][L - 1:L, :]
        b_end = b_col[u][L - 1:L, :]
        decay = jnp.exp(b_end + m_prev[u] - m_end)
        wk = jnp.exp(b_end - b_col[u] + ig_col[u] - m_end)
        kw = k[u].astype(F32) * wk
        c_ref[bi, h] = decay * c_ref[bi, h] + lax.dot_general(
            kw.astype(BF16), v, (((0,), (0,)), ((), ())), preferred_element_type=F32)
        n_ref[bi, h] = decay * n_ref[bi, h] + jnp.sum(kw, axis=0, keepdims=True)
        m_ref[bi, h] = jnp.broadcast_to(m_end, (1, LANES))


def _mlstm_core(proj, gates, batch, seq):
    nc = seq // CHUNK
    qk_cols = M_HEADS * M_QK_DIM
    v_cols = M_HEADS * M_V_DIM
    assert batch % M_BATCH == 0
    return pl.pallas_call(
        _mlstm_kernel,
        grid=(batch // M_BATCH, nc),
        in_specs=[
            pl.BlockSpec((M_BATCH, CHUNK, qk_cols), lambda b, c: (b, c, 0)),
            pl.BlockSpec((M_BATCH, CHUNK, qk_cols), lambda b, c: (b, c, 1)),
            pl.BlockSpec((M_BATCH, CHUNK, v_cols), lambda b, c: (b, c, 1)),
            pl.BlockSpec((M_BATCH, CHUNK, v_cols), lambda b, c: (b, c, 2)),
            pl.BlockSpec((M_BATCH, CHUNK, LANES), lambda b, c: (b, c, 0)),
        ],
        out_specs=pl.BlockSpec((M_BATCH, CHUNK, v_cols), lambda b, c: (b, c, 0)),
        out_shape=jax.ShapeDtypeStruct((batch, seq, v_cols), BF16),
        scratch_shapes=[
            pltpu.VMEM((M_BATCH, M_HEADS, M_QK_DIM, M_V_DIM), F32),
            pltpu.VMEM((M_BATCH, M_HEADS, 1, M_QK_DIM), F32),
            pltpu.VMEM((M_BATCH, M_HEADS, 1, LANES), F32),
        ],
        compiler_params=_compiler_params(("parallel", "arbitrary")),
        name="mlstm_core",
    )(proj, proj, proj, proj, gates)


def _mlstm_weights(mlstm_w_in, mlstm_b_gates, mlstm_w_o):
    pad = LANES - M_GATE_COLS
    w_in = _column_tile_major(mlstm_w_in[:, :, :M_MAIN_COLS].astype(BF16), MPROJ_TN)
    w_gates = jnp.pad(mlstm_w_in[:, :, M_MAIN_COLS:], ((0, 0), (0, 0), (0, pad))).astype(BF16)
    bias = jnp.pad(mlstm_b_gates.astype(F32), ((0, 0), (0, pad)))[:, None, :]
    return w_in, w_gates, bias, mlstm_w_o.astype(BF16)


def _mlstm_mixer(x, w_in, w_gates, bias, w_o, j, g, b, layer, batch, seq):
    proj, gates = _mlstm_proj(x, w_in, w_gates, bias, j)
    hid = _mlstm_core(proj.reshape(batch, seq, M_MAIN_COLS), gates.reshape(batch, seq, LANES),
                      batch, seq)
    return _outproj_ln(hid.reshape(batch * seq, D_MODEL), w_o, j, x, g, b, layer)


def kernel(x, ffn_w1, ffn_w3, ffn_w2, ln_g, ln_b, att_w_qkv, att_sinks, att_w_o,
           mlstm_w_in, mlstm_b_gates, mlstm_w_o):
    batch, seq, d = x.shape
    assert d == D_MODEL and seq % WINDOW == 0 and (batch * seq) % FFN_TM == 0
    xt = x.reshape(batch * seq, d)
    w1 = _column_tile_major(ffn_w1.astype(BF16), FFN_TF)
    w3 = _column_tile_major(ffn_w3.astype(BF16), FFN_TF)
    w2 = ffn_w2.astype(BF16)
    g = ln_g.astype(F32).reshape(DEPTH, 3, 1, d)
    b = ln_b.astype(F32).reshape(DEPTH, 3, 1, d)
    wq, wkv, att_wo = _attention_weights(att_w_qkv, att_w_o)
    m_w_in, m_w_gates, m_bias, m_wo = _mlstm_weights(mlstm_w_in, mlstm_b_gates, mlstm_w_o)
    sinks = att_sinks.astype(F32)
    for layer in range(DEPTH):
        xt = _ffn_ln(xt, w1, w3, w2, g, b, layer, 0)
        j = layer // 2
        if layer % 2 == 0:
            xt = _attention_mixer(xt, wq, wkv, att_wo, sinks[j], j, g, b, layer, batch, seq)
        else:
            xt = _mlstm_mixer(xt, m_w_in, m_w_gates, m_bias, m_wo, j, g, b, layer, batch, seq)
        xt = _ffn_ln(xt, w1, w3, w2, g, b, layer, 1)
    return xt.reshape(batch, seq, d).astype(x.dtype)
```

```python
import functools

import jax
import jax.numpy as jnp
from jax import lax
from jax.experimental import pallas as pl
from jax.experimental.pallas import tpu as pltpu

D_MODEL = 2048
DEPTH = 4
D_FF = 5632
ATT_HEADS = 32
ATT_KV_HEADS = 4
ATT_GROUP = ATT_HEADS // ATT_KV_HEADS
ATT_HEAD_DIM = D_MODEL // ATT_HEADS
WINDOW = 128
M_HEADS = 4
M_V_DIM = D_MODEL // M_HEADS
M_QK_DIM = M_V_DIM // 2
CHUNK = 128
DEEPNORM_ALPHA = (2.0 * DEPTH) ** 0.25
LN_EPS = 1e-5

LANES = 128
MASKED = -1e30
NO_KEY_DISTANCE = 1e30
VMEM_LIMIT_BYTES = 60 * 1024 * 1024

BF16 = jnp.bfloat16
F32 = jnp.float32


def _compiler_params(semantics):
    return pltpu.CompilerParams(dimension_semantics=semantics,
                                vmem_limit_bytes=VMEM_LIMIT_BYTES)


def _layer_norm(y, g, b):
    mu = jnp.mean(y, axis=-1, keepdims=True)
    yc = y - mu
    var = jnp.mean(yc * yc, axis=-1, keepdims=True)
    return yc * lax.rsqrt(var + LN_EPS) * g + b


LN_ROWS = 16


def _layer_norm_rows(o_ref, first_row, n_rows, g, b, residual=None):
    for c in range(n_rows // LN_ROWS):
        rows = pl.ds(first_row + c * LN_ROWS, LN_ROWS)
        y = o_ref[rows, :]
        if residual is not None:
            y = y + residual(rows)
        o_ref[rows, :] = _layer_norm(y, g, b)


FFN_TM = 1024
FFN_SUB = 512
FFN_TF = 512
FFN_NF = D_FF // FFN_TF
FFN_PREFETCH_F = FFN_NF // 2
assert 0 < FFN_PREFETCH_F < FFN_NF


def _ffn_kernel(x_hbm, w1_ref, w3_ref, w2_ref, g_ref, b_ref, o_ref, xf_ref, xb_ref, sem):
    i = pl.program_id(0)
    f = pl.program_id(1)

    def x_copy(block):
        return pltpu.make_async_copy(x_hbm.at[pl.ds(block * FFN_TM, FFN_TM), :], xf_ref, sem)

    last = pl.num_programs(1) - 1

    def step(is_first, is_last):
        w2 = w2_ref[...].astype(BF16)
        for r in range(FFN_TM // FFN_SUB):
            rows = pl.ds(r * FFN_SUB, FFN_SUB)
            if is_first:
                xf = xf_ref[rows, :]
                xb = xf.astype(BF16)
                xb_ref[rows, :] = xb
            else:
                xb = xb_ref[rows, :]
            h1 = jnp.dot(xb, w1_ref[...], preferred_element_type=F32)
            h3 = jnp.dot(xb, w3_ref[...], preferred_element_type=F32)
            act = (0.5 * (h1 * jax.nn.sigmoid(h1) * h3)).astype(BF16)
            part = jnp.dot(act, w2, preferred_element_type=F32)
            if is_last and r > 0:
                _layer_norm_rows(o_ref, (r - 1) * FFN_SUB, FFN_SUB, g_ref[...], b_ref[...])
            o_ref[rows, :] = (DEEPNORM_ALPHA * xf if is_first else o_ref[rows, :]) + part
        if is_last:
            _layer_norm_rows(o_ref, FFN_TM - FFN_SUB, FFN_SUB, g_ref[...], b_ref[...])

    @pl.when(f == 0)
    def _():
        @pl.when(i == 0)
        def _():
            x_copy(0).start()

        x_copy(i).wait()
        step(True, False)

    @pl.when(f == FFN_PREFETCH_F)
    def _():
        @pl.when(i + 1 < pl.num_programs(0))
        def _():
            x_copy(i + 1).start()

    @pl.when((f > 0) & (f < last))
    def _():
        step(False, False)

    @pl.when(f == last)
    def _():
        step(False, True)


def _ln_spec(layer, which):
    return pl.BlockSpec((None, None, 1, D_MODEL), lambda *_: (layer, which, 0, 0))


def _ffn_ln(x, w1, w3, w2, g, b, layer, half):
    t = x.shape[0]
    grid = (t // FFN_TM, FFN_NF)
    return pl.pallas_call(
        _ffn_kernel,
        grid=grid,
        in_specs=[
            pl.BlockSpec(memory_space=pl.ANY),
            pl.BlockSpec((None, None, D_MODEL, FFN_TF), lambda i, f: (layer, half, 0, f)),
            pl.BlockSpec((None, None, D_MODEL, FFN_TF), lambda i, f: (layer, half, 0, f)),
            pl.BlockSpec((None, None, FFN_TF, D_MODEL), lambda i, f: (layer, half, f, 0)),
            _ln_spec(layer, 2 * half),
            _ln_spec(layer, 2 * half),
        ],
        out_specs=pl.BlockSpec((FFN_TM, D_MODEL), lambda i, f: (i, 0)),
        out_shape=jax.ShapeDtypeStruct((t, D_MODEL), F32),
        scratch_shapes=[pltpu.VMEM((FFN_TM, D_MODEL), F32),
                        pltpu.VMEM((FFN_TM, D_MODEL), BF16),
                        pltpu.SemaphoreType.DMA(())],
        compiler_params=_compiler_params(("arbitrary", "arbitrary")),
        name="ffn_ln",
    )(x, w1, w3, w2, g, b)


OUT_TM = 512


OUT_SUB = 256


def _outproj_kernel(a_ref, w_ref, x_ref, g_ref, b_ref, o_ref):
    for r in range(OUT_TM // OUT_SUB):
        rows = pl.ds(r * OUT_SUB, OUT_SUB)
        o_ref[rows, :] = jnp.dot(a_ref[rows, :], w_ref[...], preferred_element_type=F32)
        _layer_norm_rows(o_ref, r * OUT_SUB, OUT_SUB, g_ref[...], b_ref[...],
                         residual=lambda ln_rows: DEEPNORM_ALPHA * x_ref[ln_rows, :])


def _outproj_ln(a, w, j, x, g, b, layer):
    t = x.shape[0]
    return pl.pallas_call(
        _outproj_kernel,
        grid=(t // OUT_TM,),
        in_specs=[
            pl.BlockSpec((OUT_TM, D_MODEL), lambda i: (i, 0)),
            pl.BlockSpec((None, D_MODEL, D_MODEL), lambda i: (j, 0, 0)),
            pl.BlockSpec((OUT_TM, D_MODEL), lambda i: (i, 0)),
            _ln_spec(layer, 1),
            _ln_spec(layer, 1),
        ],
        out_specs=pl.BlockSpec((OUT_TM, D_MODEL), lambda i: (i, 0)),
        out_shape=jax.ShapeDtypeStruct((t, D_MODEL), F32),
        compiler_params=_compiler_params(("parallel",)),
        name="outproj_ln",
    )(a, w, x, g, b)


QKV_TM = 512
ATT_KV_COLS = ATT_KV_HEADS * ATT_HEAD_DIM
ATT_PAIRS = ATT_HEADS // 2


def _qkv_kernel(x_ref, wq_ref, wk_ref, wv_ref, q_ref, k_ref, v_ref):
    xb = x_ref[...].astype(BF16)
    q_ref[...] = jnp.dot(xb, wq_ref[...], preferred_element_type=F32).astype(BF16)
    k_ref[...] = jnp.dot(xb, wk_ref[...], preferred_element_type=F32).astype(BF16)
    v_ref[...] = jnp.dot(xb, wv_ref[...], preferred_element_type=F32).astype(BF16)


def _qkv_proj(x, wq, wkv, j):
    t = x.shape[0]
    return pl.pallas_call(
        _qkv_kernel,
        grid=(t // QKV_TM,),
        in_specs=[
            pl.BlockSpec((QKV_TM, D_MODEL), lambda i: (i, 0)),
            pl.BlockSpec((None, D_MODEL, D_MODEL), lambda i: (j, 0, 0)),
            pl.BlockSpec((None, D_MODEL, ATT_KV_COLS), lambda i: (j, 0, 0)),
            pl.BlockSpec((None, D_MODEL, ATT_KV_COLS), lambda i: (j, 0, 1)),
        ],
        out_specs=[
            pl.BlockSpec((QKV_TM, D_MODEL), lambda i: (i, 0)),
            pl.BlockSpec((QKV_TM, ATT_KV_COLS), lambda i: (i, 0)),
            pl.BlockSpec((QKV_TM, ATT_KV_COLS), lambda i: (i, 0)),
        ],
        out_shape=[
            jax.ShapeDtypeStruct((t, D_MODEL), BF16),
            jax.ShapeDtypeStruct((t, ATT_KV_COLS), BF16),
            jax.ShapeDtypeStruct((t, ATT_KV_COLS), BF16),
        ],
        compiler_params=_compiler_params(("parallel",)),
        name="qkv_proj",
    )(x, wq, wkv, wkv)


def _alibi_slope(head):
    return 2.0 ** (-8.0 * (head + 1) / ATT_HEADS)


def _attn_kernel(sink_ref, q_ref, kc_ref, kp_ref, vc_ref, vp_ref, o_ref):
    n = pl.program_id(1)
    W = WINDOW
    rows = 2 * W
    qi = lax.broadcasted_iota(jnp.int32, (rows, W), 0) & (W - 1)
    j = lax.broadcasted_iota(jnp.int32, (rows, W), 1)
    from_prev = j > qi
    dist = jnp.where(from_prev, W + qi - j, qi - j).astype(F32)
    dist = jnp.where(from_prev & (n == 0), NO_KEY_DISTANCE, dist)
    is_lo_row = lax.broadcasted_iota(jnp.int32, (rows, 1), 0) < W
    lo_lane = lax.broadcasted_iota(jnp.int32, (1, LANES), 1) < ATT_HEAD_DIM
    zero = jnp.zeros((), BF16)

    for c2 in range(ATT_KV_HEADS // 2):
        cols = slice(LANES * c2, LANES * (c2 + 1))
        kk = jnp.concatenate([kp_ref[:, cols], kc_ref[:, cols]], axis=0)
        vv = jnp.concatenate([vp_ref[:, cols], vc_ref[:, cols]], axis=0)
        vcat = jnp.concatenate([jnp.where(lo_lane, vv, zero),
                                jnp.where(lo_lane, zero, vv)], axis=0)
        G = range(ATT_GROUP)
        pcols = [slice(LANES * (c2 * ATT_GROUP + g), LANES * (c2 * ATT_GROUP + g + 1)) for g in G]
        head_lo = [(2 * c2) * ATT_GROUP + g for g in G]
        head_hi = [(2 * c2 + 1) * ATT_GROUP + g for g in G]
        slope = [jnp.where(is_lo_row, _alibi_slope(head_lo[g]), _alibi_slope(head_hi[g])) for g in G]
        sink = [jnp.where(is_lo_row, sink_ref[head_lo[g]], sink_ref[head_hi[g]]) for g in G]
        qp = [q_ref[:, pcols[g]] * jnp.asarray(ATT_HEAD_DIM ** -0.5, BF16) for g in G]
        qs = [jnp.concatenate([jnp.where(lo_lane, qp[g], zero),
                               jnp.where(lo_lane, zero, qp[g])], axis=0) for g in G]
        s2 = [lax.dot_general(qs[g], kk, (((1,), (1,)), ((), ())),
                              preferred_element_type=F32) for g in G]
        s = [jnp.where(from_prev, s2[g][:, :W], s2[g][:, W:]) - slope[g] * dist for g in G]
        m = [jnp.maximum(jnp.max(s[g], axis=-1, keepdims=True), sink[g]) for g in G]
        p = [jnp.exp(s[g] - m[g]) for g in G]
        denom = [jnp.sum(p[g], axis=-1, keepdims=True) + jnp.exp(sink[g] - m[g]) for g in G]
        p = [p[g] * (1.0 / denom[g]) for g in G]
        for g in G:
            p_prev = jnp.where(from_prev, p[g], 0.0).astype(BF16)
            p_cur = jnp.where(from_prev, 0.0, p[g]).astype(BF16)
            pcat = jnp.concatenate([p_prev[:W], p_cur[:W], p_prev[W:], p_cur[W:]], axis=1)
            o = jnp.dot(pcat, vcat, preferred_element_type=F32)
            o_ref[:, pcols[g]] = o.astype(BF16)


def _attn_core(sinks, q, k, v, batch, seq):
    nb = seq // WINDOW
    cur = lambda b, n: (b * nb + n, 0)
    prev = lambda b, n: (b * nb + jnp.maximum(n - 1, 0), 0)
    return pl.pallas_call(
        _attn_kernel,
        grid=(batch, nb),
        in_specs=[
            pl.BlockSpec(memory_space=pltpu.SMEM),
            pl.BlockSpec((WINDOW, D_MODEL), cur),
            pl.BlockSpec((WINDOW, ATT_KV_COLS), cur),
            pl.BlockSpec((WINDOW, ATT_KV_COLS), prev),
            pl.BlockSpec((WINDOW, ATT_KV_COLS), cur),
            pl.BlockSpec((WINDOW, ATT_KV_COLS), prev),
        ],
        out_specs=pl.BlockSpec((WINDOW, D_MODEL), cur),
        out_shape=jax.ShapeDtypeStruct((batch * seq, D_MODEL), BF16),
        compiler_params=_compiler_params(("parallel", "arbitrary")),
        name="swa_core",
    )(sinks, q, k, k, v, v)


def _to_head_pair_order(w, axis):
    shape = w.shape
    split = shape[:axis] + (ATT_KV_HEADS // 2, 2, ATT_GROUP, ATT_HEAD_DIM) + shape[axis + 1:]
    order = list(range(len(split)))
    order[axis + 1], order[axis + 2] = order[axis + 2], order[axis + 1]
    return w.reshape(split).transpose(order).reshape(shape)


def _attention_weights(att_w_qkv, att_w_o):
    nq = ATT_HEADS * ATT_HEAD_DIM
    wq = _to_head_pair_order(att_w_qkv[:, :, :nq], 2).astype(BF16)
    wkv = att_w_qkv[:, :, nq:].astype(BF16)
    wo = _to_head_pair_order(att_w_o, 1).astype(BF16)
    return wq, wkv, wo


def _attention_mixer(x, wq, wkv, wo, sinks, j, g, b, layer, batch, seq):
    q, k, v = _qkv_proj(x, wq, wkv, j)
    o = _attn_core(sinks, q, k, v, batch, seq)
    return _outproj_ln(o, wo, j, x, g, b, layer)


MPROJ_TN = 1536
M_MAIN_COLS = 2 * M_HEADS * M_QK_DIM + 2 * M_HEADS * M_V_DIM
M_GATE_COLS = 2 * M_HEADS
M_K_COL0 = M_HEADS * M_QK_DIM
M_V_COL0 = 2 * M_HEADS * M_QK_DIM
M_OG_COL0 = M_V_COL0 + M_HEADS * M_V_DIM
M_BATCH = 2


def _log_sigmoid(x):
    return jnp.minimum(x, 0.0) - jnp.log1p(jnp.exp(-jnp.abs(x)))


def _mlstm_project(x_ref, w_ref, wg_ref, bg_ref, proj_ref, gbuf_ref, slot):
    xb = jnp.concatenate([x_ref[bi] for bi in range(M_BATCH)], axis=0).astype(BF16)
    for nt in range(M_MAIN_COLS // MPROJ_TN):
        cols = slice(nt * MPROJ_TN, (nt + 1) * MPROJ_TN)
        proj_ref[slot, :, cols] = jnp.dot(xb, w_ref[:, cols], preferred_element_type=F32).astype(BF16)
    gbuf_ref[slot] = jnp.dot(xb, wg_ref[...], preferred_element_type=F32) + bg_ref[...]


def _mlstm_chunk(proj_ref, gbuf_ref, slot, o_ref, out_row0, c_ref, n_ref, m_ref):
    L = CHUNK
    t_idx = lax.broadcasted_iota(jnp.int32, (L, L), 0)
    s_idx = lax.broadcasted_iota(jnp.int32, (L, L), 1)
    causal = s_idx <= t_idx

    units = [(bi, h) for bi in range(M_BATCH) for h in range(M_HEADS)]
    rows = [slice(bi * L, (bi + 1) * L) for bi, _ in units]
    out_rows = slice(out_row0, out_row0 + L)
    qk_cols = [slice(h * M_QK_DIM, (h + 1) * M_QK_DIM) for _, h in units]
    k_cols = [slice(M_K_COL0 + h * M_QK_DIM, M_K_COL0 + (h + 1) * M_QK_DIM) for _, h in units]
    v_cols = [slice(h * M_V_DIM, (h + 1) * M_V_DIM) for _, h in units]
    pv_cols = [slice(M_V_COL0 + h * M_V_DIM, M_V_COL0 + (h + 1) * M_V_DIM) for _, h in units]
    og_cols = [slice(M_OG_COL0 + h * M_V_DIM, M_OG_COL0 + (h + 1) * M_V_DIM) for _, h in units]
    gates = [gbuf_ref[slot, bi * L:(bi + 1) * L, :] for bi in range(M_BATCH)]
    gates_t = [gt.T for gt in gates]

    ig_col, b_col, dmat, row_max = [], [], [], []
    for bi, h in units:
        ig_col.append(gates[bi][:, h:h + 1])
        ig_row = gates_t[bi][h:h + 1, :]
        lf_col = _log_sigmoid(gates[bi][:, M_HEADS + h:M_HEADS + h + 1])
        lf_row = _log_sigmoid(gates_t[bi][M_HEADS + h:M_HEADS + h + 1, :])
        bc = jnp.sum(jnp.where(causal, lf_row, 0.0), axis=1, keepdims=True)
        br = jnp.sum(jnp.where(t_idx <= s_idx, lf_col, 0.0), axis=0, keepdims=True)
        d = jnp.where(causal, bc - br + ig_row, MASKED)
        b_col.append(bc)
        dmat.append(d)
        row_max.append(jnp.max(d, axis=1, keepdims=True))

    q = [proj_ref[slot, rows[u], qk_cols[u]] for u in range(len(units))]
    k = [proj_ref[slot, rows[u], k_cols[u]] * jnp.asarray(M_QK_DIM ** -0.5, BF16)
         for u in range(len(units))]
    qk = [lax.dot_general(q[u], k[u], (((1,), (1,)), ((), ())), preferred_element_type=F32)
          for u in range(len(units))]
    qc = [jnp.dot(q[u], c_ref[bi, h].astype(BF16), preferred_element_type=F32)
          for u, (bi, h) in enumerate(units)]

    m_prev = [m_ref[bi, h][:, 0:1] for bi, h in units]
    m_rows = [jnp.maximum(b_col[u] + m_prev[u], row_max[u]) for u in range(len(units))]
    inter = [jnp.exp(b_col[u] + m_prev[u] - m_rows[u]) for u in range(len(units))]
    w = [qk[u] * jnp.exp(dmat[u] - m_rows[u]) for u in range(len(units))]

    for u, (bi, h) in enumerate(units):
        v = proj_ref[slot, rows[u], pv_cols[u]]
        num = inter[u] * qc[u] + jnp.dot(w[u].astype(BF16), v, preferred_element_type=F32)
        qn = jnp.sum(q[u].astype(F32) * n_ref[bi, h], axis=1, keepdims=True)
        den = inter[u] * qn + jnp.sum(w[u], axis=1, keepdims=True)
        hid = num * (1.0 / jnp.maximum(jnp.abs(den), jnp.exp(-m_rows[u])))
        og = proj_ref[slot, rows[u], og_cols[u]].astype(F32)
        o_ref[bi, out_rows, v_cols[u]] = (hid * jax.nn.sigmoid(og)).astype(BF16)

    for u, (bi, h) in enumerate(units):
        v = proj_ref[slot, rows[u], pv_cols[u]]
        m_end = m_rows[u][L - 1:L, :]
        b_end = b_col[u][L - 1:L, :]
        decay = jnp.exp(b_end + m_prev[u] - m_end)
        wk = jnp.exp(b_end - b_col[u] + ig_col[u] - m_end)
        kw = k[u].astype(F32) * wk
        c_ref[bi, h] = decay * c_ref[bi, h] + lax.dot_general(
            kw.astype(BF16), v, (((0,), (0,)), ((), ())), preferred_element_type=F32)
        n_ref[bi, h] = decay * n_ref[bi, h] + jnp.sum(kw, axis=0, keepdims=True)
        m_ref[bi, h] = jnp.broadcast_to(m_end, (1, LANES))


def _mlstm_kernel(xa_ref, xb_ref, w_ref, wg_ref, bg_ref, o_ref,
                  proj_ref, gbuf_ref, c_ref, n_ref, m_ref):
    s = pl.program_id(1)
    last = pl.num_programs(1) - 1
    L = CHUNK

    def project(x_ref, slot):
        _mlstm_project(x_ref, w_ref, wg_ref, bg_ref, proj_ref, gbuf_ref, slot)

    def chunk(slot, out_row0):
        _mlstm_chunk(proj_ref, gbuf_ref, slot, o_ref, out_row0, c_ref, n_ref, m_ref)

    @pl.when(s == 0)
    def _():
        c_ref[...] = jnp.zeros_like(c_ref)
        n_ref[...] = jnp.zeros_like(n_ref)
        m_ref[...] = jnp.zeros_like(m_ref)
        project(xb_ref, 0)

    @pl.when((s > 0) & (s < last))
    def _():
        project(xa_ref, 1)
        chunk(0, 0)
        project(xb_ref, 0)
        chunk(1, L)

    @pl.when(s == last)
    def _():
        project(xa_ref, 1)
        chunk(0, 0)
        chunk(1, L)


def _mlstm_proj_core(x, w_in, w_gates, b_gates, j, batch, seq):
    nc = seq // CHUNK
    v_cols = M_HEADS * M_V_DIM
    assert batch % M_BATCH == 0 and nc % 2 == 0
    const = lambda b, s: (j, 0, 0)
    return pl.pallas_call(
        _mlstm_kernel,
        grid=(batch // M_BATCH, nc // 2 + 1),
        in_specs=[
            pl.BlockSpec((M_BATCH, CHUNK, D_MODEL), lambda b, s: (b, jnp.maximum(2 * s - 1, 0), 0)),
            pl.BlockSpec((M_BATCH, CHUNK, D_MODEL), lambda b, s: (b, jnp.minimum(2 * s, nc - 1), 0)),
            pl.BlockSpec((None, D_MODEL, M_MAIN_COLS), const, pipeline_mode=pl.Buffered(1)),
            pl.BlockSpec((None, D_MODEL, LANES), const),
            pl.BlockSpec((None, 1, LANES), const),
        ],
        out_specs=pl.BlockSpec((M_BATCH, 2 * CHUNK, v_cols), lambda b, s: (b, jnp.maximum(s - 1, 0), 0)),
        out_shape=jax.ShapeDtypeStruct((batch, seq, v_cols), BF16),
        scratch_shapes=[
            pltpu.VMEM((2, M_BATCH * CHUNK, M_MAIN_COLS), BF16),
            pltpu.VMEM((2, M_BATCH * CHUNK, LANES), F32),
            pltpu.VMEM((M_BATCH, M_HEADS, M_QK_DIM, M_V_DIM), F32),
            pltpu.VMEM((M_BATCH, M_HEADS, 1, M_QK_DIM), F32),
            pltpu.VMEM((M_BATCH, M_HEADS, 1, LANES), F32),
        ],
        compiler_params=_compiler_params(("parallel", "arbitrary")),
        name="mlstm_proj_core",
    )(x, x, w_in, w_gates, b_gates)


def _mlstm_weights(mlstm_w_in, mlstm_b_gates, mlstm_w_o):
    pad = LANES - M_GATE_COLS
    w_in = mlstm_w_in.astype(BF16)
    w_gates = jnp.pad(mlstm_w_in[:, :, M_MAIN_COLS:], ((0, 0), (0, 0), (0, pad))).astype(BF16)
    bias = jnp.pad(mlstm_b_gates.astype(F32), ((0, 0), (0, pad)))[:, None, :]
    return w_in, w_gates, bias, mlstm_w_o.astype(BF16)


def _mlstm_mixer(x, w_in, w_gates, bias, w_o, j, g, b, layer, batch, seq):
    hid = _mlstm_proj_core(x.reshape(batch, seq, D_MODEL), w_in, w_gates, bias, j, batch, seq)
    return _outproj_ln(hid.reshape(batch * seq, D_MODEL), w_o, j, x, g, b, layer)


def kernel(x, ffn_w1, ffn_w3, ffn_w2, ln_g, ln_b, att_w_qkv, att_sinks, att_w_o,
           mlstm_w_in, mlstm_b_gates, mlstm_w_o):
    batch, seq, d = x.shape
    assert d == D_MODEL and seq % WINDOW == 0 and (batch * seq) % FFN_TM == 0
    xt = x.reshape(batch * seq, d)
    w1 = ffn_w1.astype(BF16)
    w3 = ffn_w3.astype(BF16)
    w2 = ffn_w2.astype(F32)
    g = ln_g.astype(F32).reshape(DEPTH, 3, 1, d)
    b = ln_b.astype(F32).reshape(DEPTH, 3, 1, d)
    wq, wkv, att_wo = _attention_weights(att_w_qkv, att_w_o)
    m_w_in, m_w_gates, m_bias, m_wo = _mlstm_weights(mlstm_w_in, mlstm_b_gates, mlstm_w_o)
    sinks = att_sinks.astype(F32)
    for layer in range(DEPTH):
        xt = _ffn_ln(xt, w1, w3, w2, g, b, layer, 0)
        j = layer // 2
        if layer % 2 == 0:
            xt = _attention_mixer(xt, wq, wkv, att_wo, sinks[j], j, g, b, layer, batch, seq)
        else:
            xt = _mlstm_mixer(xt, m_w_in, m_w_gates, m_bias, m_wo, j, g, b, layer, batch, seq)
        xt = _ffn_ln(xt, w1, w3, w2, g, b, layer, 1)
    return xt.reshape(batch, seq, d).astype(x.dtype)
```

```python
import functools

import jax
import jax.numpy as jnp
from jax import lax
from jax.experimental import pallas as pl
from jax.experimental.pallas import tpu as pltpu

D_MODEL = 2048
DEPTH = 4
D_FF = 5632
ATT_HEADS = 32
ATT_KV_HEADS = 4
ATT_GROUP = ATT_HEADS // ATT_KV_HEADS
ATT_HEAD_DIM = D_MODEL // ATT_HEADS
WINDOW = 128
M_HEADS = 4
M_V_DIM = D_MODEL // M_HEADS
M_QK_DIM = M_V_DIM // 2
CHUNK = 128
DEEPNORM_ALPHA = (2.0 * DEPTH) ** 0.25
LN_EPS = 1e-5

LANES = 128
MASKED = -1e30
NO_KEY_DISTANCE = 1e30
VMEM_LIMIT_BYTES = 60 * 1024 * 1024

BF16 = jnp.bfloat16
F32 = jnp.float32


def _compiler_params(semantics):
    return pltpu.CompilerParams(dimension_semantics=semantics,
                                vmem_limit_bytes=VMEM_LIMIT_BYTES)


def _layer_norm(y, g, b):
    mu = jnp.mean(y, axis=-1, keepdims=True)
    yc = y - mu
    var = jnp.mean(yc * yc, axis=-1, keepdims=True)
    return yc * lax.rsqrt(var + LN_EPS) * g + b


LN_ROWS = 16


def _layer_norm_rows(o_ref, first_row, n_rows, g, b, residual=None):
    for c in range(n_rows // LN_ROWS):
        rows = pl.ds(first_row + c * LN_ROWS, LN_ROWS)
        y = o_ref[rows, :]
        if residual is not None:
            y = y + residual(rows)
        o_ref[rows, :] = _layer_norm(y, g, b)


FFN_TM = 1024
FFN_SUB = 512
FFN_TF = 512
FFN_NF = D_FF // FFN_TF
FFN_PREFETCH_F = FFN_NF // 2
assert 0 < FFN_PREFETCH_F < FFN_NF


def _ffn_kernel(x_hbm, w1_ref, w3_ref, w2_ref, g_ref, b_ref, o_ref, xf_ref, xb_ref, sem):
    i = pl.program_id(0)
    f = pl.program_id(1)

    def x_copy(block):
        return pltpu.make_async_copy(x_hbm.at[pl.ds(block * FFN_TM, FFN_TM), :], xf_ref, sem)

    last = pl.num_programs(1) - 1

    def step(is_first, is_last):
        w2 = w2_ref[...].astype(BF16)
        for r in range(FFN_TM // FFN_SUB):
            rows = pl.ds(r * FFN_SUB, FFN_SUB)
            if is_first:
                xf = xf_ref[rows, :]
                xb = xf.astype(BF16)
                xb_ref[rows, :] = xb
            else:
                xb = xb_ref[rows, :]
            h1 = jnp.dot(xb, w1_ref[...], preferred_element_type=F32)
            h3 = jnp.dot(xb, w3_ref[...], preferred_element_type=F32)
            act = (0.5 * (h1 * jax.nn.sigmoid(h1) * h3)).astype(BF16)
            part = jnp.dot(act, w2, preferred_element_type=F32)
            if is_last and r > 0:
                _layer_norm_rows(o_ref, (r - 1) * FFN_SUB, FFN_SUB, g_ref[...], b_ref[...])
            o_ref[rows, :] = (DEEPNORM_ALPHA * xf if is_first else o_ref[rows, :]) + part
        if is_last:
            _layer_norm_rows(o_ref, FFN_TM - FFN_SUB, FFN_SUB, g_ref[...], b_ref[...])

    @pl.when(f == 0)
    def _():
        @pl.when(i == 0)
        def _():
            x_copy(0).start()

        x_copy(i).wait()
        step(True, False)

    @pl.when(f == FFN_PREFETCH_F)
    def _():
        @pl.when(i + 1 < pl.num_programs(0))
        def _():
            x_copy(i + 1).start()

    @pl.when((f > 0) & (f < last))
    def _():
        step(False, False)

    @pl.when(f == last)
    def _():
        step(False, True)


def _ln_spec(layer, which):
    return pl.BlockSpec((None, None, 1, D_MODEL), lambda *_: (layer, which, 0, 0))


def _ffn_ln(x, w1, w3, w2, g, b, layer, half):
    t = x.shape[0]
    grid = (t // FFN_TM, FFN_NF)
    return pl.pallas_call(
        _ffn_kernel,
        grid=grid,
        in_specs=[
            pl.BlockSpec(memory_space=pl.ANY),
            pl.BlockSpec((None, None, D_MODEL, FFN_TF), lambda i, f: (layer, half, 0, f)),
            pl.BlockSpec((None, None, D_MODEL, FFN_TF), lambda i, f: (layer, half, 0, f)),
            pl.BlockSpec((None, None, FFN_TF, D_MODEL), lambda i, f: (layer, half, f, 0)),
            _ln_spec(layer, 2 * half),
            _ln_spec(layer, 2 * half),
        ],
        out_specs=pl.BlockSpec((FFN_TM, D_MODEL), lambda i, f: (i, 0)),
        out_shape=jax.ShapeDtypeStruct((t, D_MODEL), F32),
        scratch_shapes=[pltpu.VMEM((FFN_TM, D_MODEL), F32),
                        pltpu.VMEM((FFN_TM, D_MODEL), BF16),
                        pltpu.SemaphoreType.DMA(())],
        compiler_params=_compiler_params(("arbitrary", "arbitrary")),
        name="ffn_ln",
    )(x, w1, w3, w2, g, b)


OUT_TM = 512


OUT_SUB = 256


def _outproj_kernel(a_ref, w_ref, x_ref, g_ref, b_ref, o_ref):
    for r in range(OUT_TM // OUT_SUB):
        rows = pl.ds(r * OUT_SUB, OUT_SUB)
        o_ref[rows, :] = jnp.dot(a_ref[rows, :], w_ref[...], preferred_element_type=F32)
        _layer_norm_rows(o_ref, r * OUT_SUB, OUT_SUB, g_ref[...], b_ref[...],
                         residual=lambda ln_rows: DEEPNORM_ALPHA * x_ref[ln_rows, :])


def _outproj_ln(a, w, j, x, g, b, layer):
    t = x.shape[0]
    return pl.pallas_call(
        _outproj_kernel,
        grid=(t // OUT_TM,),
        in_specs=[
            pl.BlockSpec((OUT_TM, D_MODEL), lambda i: (i, 0)),
            pl.BlockSpec((None, D_MODEL, D_MODEL), lambda i: (j, 0, 0)),
            pl.BlockSpec((OUT_TM, D_MODEL), lambda i: (i, 0)),
            _ln_spec(layer, 1),
            _ln_spec(layer, 1),
        ],
        out_specs=pl.BlockSpec((OUT_TM, D_MODEL), lambda i: (i, 0)),
        out_shape=jax.ShapeDtypeStruct((t, D_MODEL), F32),
        compiler_params=_compiler_params(("parallel",)),
        name="outproj_ln",
    )(a, w, x, g, b)


ATT_KV_COLS = ATT_KV_HEADS * ATT_HEAD_DIM
ATT_SUB = 2 * WINDOW
ATT_K_COL0 = D_MODEL
ATT_V_COL0 = D_MODEL + ATT_KV_COLS
ATT_SLOT_COLS = D_MODEL + 2 * ATT_KV_COLS


def _alibi_slope(head):
    return 2.0 ** (-8.0 * (head + 1) / ATT_HEADS)


def _attn_project(x_ref, wq_ref, wkv_ref, qkv_ref, slot):
    xb = x_ref[...].astype(BF16)
    qkv_ref[slot, :, :D_MODEL] = jnp.dot(xb, wq_ref[...], preferred_element_type=F32).astype(BF16)
    qkv_ref[slot, :, D_MODEL:] = jnp.dot(xb, wkv_ref[...], preferred_element_type=F32).astype(BF16)


def _attend_block(sink_ref, q_at, kp_at, kc_at, vp_at, vc_at, no_prev, o_ref, out_rows):
    W = WINDOW
    rows = 2 * W
    qi = lax.broadcasted_iota(jnp.int32, (rows, W), 0) & (W - 1)
    j = lax.broadcasted_iota(jnp.int32, (rows, W), 1)
    from_prev = j > qi
    dist = jnp.where(from_prev, W + qi - j, qi - j).astype(F32)
    dist = jnp.where(from_prev & no_prev, NO_KEY_DISTANCE, dist)
    is_lo_row = lax.broadcasted_iota(jnp.int32, (rows, 1), 0) < W
    lo_lane = lax.broadcasted_iota(jnp.int32, (1, LANES), 1) < ATT_HEAD_DIM
    zero = jnp.zeros((), BF16)

    for c2 in range(ATT_KV_HEADS // 2):
        kk = jnp.concatenate([kp_at(c2), kc_at(c2)], axis=0)
        vv = jnp.concatenate([vp_at(c2), vc_at(c2)], axis=0)
        vcat = jnp.concatenate([jnp.where(lo_lane, vv, zero),
                                jnp.where(lo_lane, zero, vv)], axis=0)
        G = range(ATT_GROUP)
        pair = [c2 * ATT_GROUP + g for g in G]
        head_lo = [(2 * c2) * ATT_GROUP + g for g in G]
        head_hi = [(2 * c2 + 1) * ATT_GROUP + g for g in G]
        slope = [jnp.where(is_lo_row, _alibi_slope(head_lo[g]), _alibi_slope(head_hi[g])) for g in G]
        sink = [jnp.where(is_lo_row, sink_ref[head_lo[g]], sink_ref[head_hi[g]]) for g in G]
        qp = [q_at(pair[g]) * jnp.asarray(ATT_HEAD_DIM ** -0.5, BF16) for g in G]
        qs = [jnp.concatenate([jnp.where(lo_lane, qp[g], zero),
                               jnp.where(lo_lane, zero, qp[g])], axis=0) for g in G]
        s2_all = lax.dot_general(jnp.concatenate(qs, axis=0), kk, (((1,), (1,)), ((), ())),
                                 preferred_element_type=F32)
        s2 = [s2_all[g * rows:(g + 1) * rows] for g in G]
        s = [jnp.where(from_prev, s2[g][:, :W], s2[g][:, W:]) - slope[g] * dist for g in G]
        m = [jnp.maximum(jnp.max(s[g], axis=-1, keepdims=True), sink[g]) for g in G]
        p = [jnp.exp(s[g] - m[g]) for g in G]
        denom = [jnp.sum(p[g], axis=-1, keepdims=True) + jnp.exp(sink[g] - m[g]) for g in G]
        p = [p[g] * (1.0 / denom[g]) for g in G]
        pcat = []
        for g in G:
            p_prev = jnp.where(from_prev, p[g], 0.0).astype(BF16)
            p_cur = jnp.where(from_prev, 0.0, p[g]).astype(BF16)
            pcat.append(jnp.concatenate([p_prev[:W], p_cur[:W], p_prev[W:], p_cur[W:]], axis=1))
        o_all = jnp.dot(jnp.concatenate(pcat, axis=0), vcat, preferred_element_type=F32)
        for g in G:
            o_ref[out_rows, LANES * pair[g]:LANES * (pair[g] + 1)] = o_all[g * W:(g + 1) * W].astype(BF16)


def _attn_kernel(sink_ref, xa_ref, xb_ref, wq_ref, wkv_ref, o_ref, qkv_ref, carry_ref):
    s = pl.program_id(1)
    last = pl.num_programs(1) - 1
    W = WINDOW

    def project(x_ref, slot):
        _attn_project(x_ref, wq_ref, wkv_ref, qkv_ref, slot)

    def attend(slot, out_row0, no_prev):
        lanes = lambda c2: slice(LANES * c2, LANES * (c2 + 1))
        first, second = slice(0, W), slice(W, 2 * W)
        slot_at = lambda rows, col0: (
            lambda c2: qkv_ref[slot, rows, col0 + LANES * c2:col0 + LANES * (c2 + 1)])
        _attend_block(sink_ref,
                      lambda pair: qkv_ref[slot, first, lanes(pair)],
                      lambda c2: carry_ref[:, lanes(c2)],
                      slot_at(first, ATT_K_COL0),
                      lambda c2: carry_ref[:, ATT_KV_COLS + LANES * c2:ATT_KV_COLS + LANES * (c2 + 1)],
                      slot_at(first, ATT_V_COL0),
                      no_prev, o_ref, slice(out_row0, out_row0 + W))
        _attend_block(sink_ref,
                      lambda pair: qkv_ref[slot, second, lanes(pair)],
                      slot_at(first, ATT_K_COL0), slot_at(second, ATT_K_COL0),
                      slot_at(first, ATT_V_COL0), slot_at(second, ATT_V_COL0),
                      False, o_ref, slice(out_row0 + W, out_row0 + 2 * W))
        carry_ref[...] = qkv_ref[slot, second, ATT_K_COL0:]

    @pl.when(s == 0)
    def _():
        carry_ref[...] = jnp.zeros_like(carry_ref)
        project(xb_ref, 0)

    @pl.when((s > 0) & (s < last))
    def _():
        project(xa_ref, 1)
        attend(0, 0, s == 1)
        project(xb_ref, 0)
        attend(1, ATT_SUB, False)

    @pl.when(s == last)
    def _():
        project(xa_ref, 1)
        attend(0, 0, s == 1)
        attend(1, ATT_SUB, False)


def _attn_proj_core(sinks, x, wq, wkv, j, batch, seq):
    nu = seq // ATT_SUB
    assert nu % 2 == 0
    const = lambda b, s: (j, 0, 0)
    return pl.pallas_call(
        _attn_kernel,
        grid=(batch, nu // 2 + 1),
        in_specs=[
            pl.BlockSpec(memory_space=pltpu.SMEM),
            pl.BlockSpec((None, ATT_SUB, D_MODEL), lambda b, s: (b, jnp.maximum(2 * s - 1, 0), 0)),
            pl.BlockSpec((None, ATT_SUB, D_MODEL), lambda b, s: (b, jnp.minimum(2 * s, nu - 1), 0)),
            pl.BlockSpec((None, D_MODEL, D_MODEL), const, pipeline_mode=pl.Buffered(1)),
            pl.BlockSpec((None, D_MODEL, 2 * ATT_KV_COLS), const, pipeline_mode=pl.Buffered(1)),
        ],
        out_specs=pl.BlockSpec((None, 2 * ATT_SUB, D_MODEL), lambda b, s: (b, jnp.maximum(s - 1, 0), 0)),
        out_shape=jax.ShapeDtypeStruct((batch, seq, D_MODEL), BF16),
        scratch_shapes=[
            pltpu.VMEM((2, ATT_SUB, ATT_SLOT_COLS), BF16),
            pltpu.VMEM((WINDOW, 2 * ATT_KV_COLS), BF16),
        ],
        compiler_params=_compiler_params(("parallel", "arbitrary")),
        name="swa_proj_core",
    )(sinks, x, x, wq, wkv)


def _to_head_pair_order(w, axis):
    shape = w.shape
    split = shape[:axis] + (ATT_KV_HEADS // 2, 2, ATT_GROUP, ATT_HEAD_DIM) + shape[axis + 1:]
    order = list(range(len(split)))
    order[axis + 1], order[axis + 2] = order[axis + 2], order[axis + 1]
    return w.reshape(split).transpose(order).reshape(shape)


def _attention_weights(att_w_qkv, att_w_o):
    nq = ATT_HEADS * ATT_HEAD_DIM
    wq = _to_head_pair_order(att_w_qkv[:, :, :nq], 2).astype(BF16)
    wkv = att_w_qkv[:, :, nq:].astype(BF16)
    wo = _to_head_pair_order(att_w_o, 1).astype(BF16)
    return wq, wkv, wo


def _attention_mixer(x, wq, wkv, wo, sinks, j, g, b, layer, batch, seq):
    o = _attn_proj_core(sinks, x.reshape(batch, seq, D_MODEL), wq, wkv, j, batch, seq)
    return _outproj_ln(o.reshape(batch * seq, D_MODEL), wo, j, x, g, b, layer)


MPROJ_TN = 1536
M_MAIN_COLS = 2 * M_HEADS * M_QK_DIM + 2 * M_HEADS * M_V_DIM
M_GATE_COLS = 2 * M_HEADS
M_K_COL0 = M_HEADS * M_QK_DIM
M_V_COL0 = 2 * M_HEADS * M_QK_DIM
M_OG_COL0 = M_V_COL0 + M_HEADS * M_V_DIM
M_BATCH = 2


def _log_sigmoid(x):
    return jnp.minimum(x, 0.0) - jnp.log1p(jnp.exp(-jnp.abs(x)))


def _mlstm_project(x_ref, w_ref, wg_ref, bg_ref, proj_ref, gbuf_ref, slot):
    xb = jnp.concatenate([x_ref[bi] for bi in range(M_BATCH)], axis=0).astype(BF16)
    for nt in range(M_MAIN_COLS // MPROJ_TN):
        cols = slice(nt * MPROJ_TN, (nt + 1) * MPROJ_TN)
        proj_ref[slot, :, cols] = jnp.dot(xb, w_ref[:, cols], preferred_element_type=F32).astype(BF16)
    gbuf_ref[slot] = jnp.dot(xb, wg_ref[...], preferred_element_type=F32) + bg_ref[...]


def _mlstm_chunk(proj_ref, gbuf_ref, slot, o_ref, out_row0, c_ref, n_ref, m_ref):
    L = CHUNK
    t_idx = lax.broadcasted_iota(jnp.int32, (L, L), 0)
    s_idx = lax.broadcasted_iota(jnp.int32, (L, L), 1)
    causal = s_idx <= t_idx

    units = [(bi, h) for bi in range(M_BATCH) for h in range(M_HEADS)]
    rows = [slice(bi * L, (bi + 1) * L) for bi, _ in units]
    out_rows = slice(out_row0, out_row0 + L)
    qk_cols = [slice(h * M_QK_DIM, (h + 1) * M_QK_DIM) for _, h in units]
    k_cols = [slice(M_K_COL0 + h * M_QK_DIM, M_K_COL0 + (h + 1) * M_QK_DIM) for _, h in units]
    v_cols = [slice(h * M_V_DIM, (h + 1) * M_V_DIM) for _, h in units]
    pv_cols = [slice(M_V_COL0 + h * M_V_DIM, M_V_COL0 + (h + 1) * M_V_DIM) for _, h in units]
    og_cols = [slice(M_OG_COL0 + h * M_V_DIM, M_OG_COL0 + (h + 1) * M_V_DIM) for _, h in units]
    gates = [gbuf_ref[slot, bi * L:(bi + 1) * L, :] for bi in range(M_BATCH)]
    gates_t = [gt.T for gt in gates]

    ig_col, b_col, dmat, row_max = [], [], [], []
    for bi, h in units:
        ig_col.append(gates[bi][:, h:h + 1])
        ig_row = gates_t[bi][h:h + 1, :]
        lf_col = _log_sigmoid(gates[bi][:, M_HEADS + h:M_HEADS + h + 1])
        lf_row = _log_sigmoid(gates_t[bi][M_HEADS + h:M_HEADS + h + 1, :])
        bc = jnp.sum(jnp.where(causal, lf_row, 0.0), axis=1, keepdims=True)
        br = jnp.sum(jnp.where(t_idx <= s_idx, lf_col, 0.0), axis=0, keepdims=True)
        d = jnp.where(causal, bc - br + ig_row, MASKED)
        b_col.append(bc)
        dmat.append(d)
        row_max.append(jnp.max(d, axis=1, keepdims=True))

    q = [proj_ref[slot, rows[u], qk_cols[u]] for u in range(len(units))]
    k = [proj_ref[slot, rows[u], k_cols[u]] * jnp.asarray(M_QK_DIM ** -0.5, BF16)
         for u in range(len(units))]
    qk = [lax.dot_general(q[u], k[u], (((1,), (1,)), ((), ())), preferred_element_type=F32)
          for u in range(len(units))]
    qc = [jnp.dot(q[u], c_ref[bi, h].astype(BF16), preferred_element_type=F32)
          for u, (bi, h) in enumerate(units)]

    m_prev = [m_ref[bi, h][:, 0:1] for bi, h in units]
    m_rows = [jnp.maximum(b_col[u] + m_prev[u], row_max[u]) for u in range(len(units))]
    inter = [jnp.exp(b_col[u] + m_prev[u] - m_rows[u]) for u in range(len(units))]
    w = [qk[u] * jnp.exp(dmat[u] - m_rows[u]) for u in range(len(units))]

    for u, (bi, h) in enumerate(units):
        v = proj_ref[slot, rows[u], pv_cols[u]]
        num = inter[u] * qc[u] + jnp.dot(w[u].astype(BF16), v, preferred_element_type=F32)
        qn = jnp.sum(q[u].astype(F32) * n_ref[bi, h], axis=1, keepdims=True)
        den = inter[u] * qn + jnp.sum(w[u], axis=1, keepdims=True)
        hid = num * (1.0 / jnp.maximum(jnp.abs(den), jnp.exp(-m_rows[u])))
        og = proj_ref[slot, rows[u], og_cols[u]].astype(F32)
        o_ref[bi, out_rows, v_cols[u]] = (hid * jax.nn.sigmoid(og)).astype(BF16)

    for u, (bi, h) in enumerate(units):
        v = proj_ref[slot, rows[u], pv_cols[u]]
        m_end = m_rows[u][L - 1:L, :]
        b_end = b_col[u][L - 1:L, :]
        decay = jnp.exp(b_end + m_prev[u] - m_end)
        wk = jnp.exp(b_end - b_col[u] + ig_col[u] - m_end)
        kw = k[u].astype(F32) * wk
        c_ref[bi, h] = decay * c_ref[bi, h] + lax.dot_general(
            kw.astype(BF16), v, (((0,), (0,)), ((), ())), preferred_element_type=F32)
        n_ref[bi, h] = decay * n_ref[bi, h] + jnp.sum(kw, axis=0, keepdims=True)
        m_ref[bi, h] = jnp.broadcast_to(m_end, (1, LANES))


def _mlstm_kernel(xa_ref, xb_ref, w_ref, wg_ref, bg_ref, o_ref,
                  proj_ref, gbuf_ref, c_ref, n_ref, m_ref):
    s = pl.program_id(1)
    last = pl.num_programs(1) - 1
    L = CHUNK

    def project(x_ref, slot):
        _mlstm_project(x_ref, w_ref, wg_ref, bg_ref, proj_ref, gbuf_ref, slot)

    def chunk(slot, out_row0):
        _mlstm_chunk(proj_ref, gbuf_ref, slot, o_ref, out_row0, c_ref, n_ref, m_ref)

    @pl.when(s == 0)
    def _():
        c_ref[...] = jnp.zeros_like(c_ref)
        n_ref[...] = jnp.zeros_like(n_ref)
        m_ref[...] = jnp.zeros_like(m_ref)
        project(xb_ref, 0)

    @pl.when((s > 0) & (s < last))
    def _():
        project(xa_ref, 1)
        chunk(0, 0)
        project(xb_ref, 0)
        chunk(1, L)

    @pl.when(s == last)
    def _():
        project(xa_ref, 1)
        chunk(0, 0)
        chunk(1, L)


def _mlstm_proj_core(x, w_in, w_gates, b_gates, j, batch, seq):
    nc = seq // CHUNK
    v_cols = M_HEADS * M_V_DIM
    assert batch % M_BATCH == 0 and nc % 2 == 0
    const = lambda b, s: (j, 0, 0)
    return pl.pallas_call(
        _mlstm_kernel,
        grid=(batch // M_BATCH, nc // 2 + 1),
        in_specs=[
            pl.BlockSpec((M_BATCH, CHUNK, D_MODEL), lambda b, s: (b, jnp.maximum(2 * s - 1, 0), 0)),
            pl.BlockSpec((M_BATCH, CHUNK, D_MODEL), lambda b, s: (b, jnp.minimum(2 * s, nc - 1), 0)),
            pl.BlockSpec((None, D_MODEL, M_MAIN_COLS), const, pipeline_mode=pl.Buffered(1)),
            pl.BlockSpec((None, D_MODEL, LANES), const),
            pl.BlockSpec((None, 1, LANES), const),
        ],
        out_specs=pl.BlockSpec((M_BATCH, 2 * CHUNK, v_cols), lambda b, s: (b, jnp.maximum(s - 1, 0), 0)),
        out_shape=jax.ShapeDtypeStruct((batch, seq, v_cols), BF16),
        scratch_shapes=[
            pltpu.VMEM((2, M_BATCH * CHUNK, M_MAIN_COLS), BF16),
            pltpu.VMEM((2, M_BATCH * CHUNK, LANES), F32),
            pltpu.VMEM((M_BATCH, M_HEADS, M_QK_DIM, M_V_DIM), F32),
            pltpu.VMEM((M_BATCH, M_HEADS, 1, M_QK_DIM), F32),
            pltpu.VMEM((M_BATCH, M_HEADS, 1, LANES), F32),
        ],
        compiler_params=_compiler_params(("parallel", "arbitrary")),
        name="mlstm_proj_core",
    )(x, x, w_in, w_gates, b_gates)


def _mlstm_weights(mlstm_w_in, mlstm_b_gates, mlstm_w_o):
    pad = LANES - M_GATE_COLS
    w_in = mlstm_w_in.astype(BF16)
    w_gates = jnp.pad(mlstm_w_in[:, :, M_MAIN_COLS:], ((0, 0), (0, 0), (0, pad))).astype(BF16)
    bias = jnp.pad(mlstm_b_gates.astype(F32), ((0, 0), (0, pad)))[:, None, :]
    return w_in, w_gates, bias, mlstm_w_o.astype(BF16)


def _mlstm_mixer(x, w_in, w_gates, bias, w_o, j, g, b, layer, batch, seq):
    hid = _mlstm_proj_core(x.reshape(batch, seq, D_MODEL), w_in, w_gates, bias, j, batch, seq)
    return _outproj_ln(hid.reshape(batch * seq, D_MODEL), w_o, j, x, g, b, layer)


def kernel(x, ffn_w1, ffn_w3, ffn_w2, ln_g, ln_b, att_w_qkv, att_sinks, att_w_o,
           mlstm_w_in, mlstm_b_gates, mlstm_w_o):
    batch, seq, d = x.shape
    assert d == D_MODEL and seq % WINDOW == 0 and (batch * seq) % FFN_TM == 0
    xt = x.reshape(batch * seq, d)
    w1 = ffn_w1.astype(BF16)
    w3 = ffn_w3.astype(BF16)
    w2 = ffn_w2.astype(F32)
    g = ln_g.astype(F32).reshape(DEPTH, 3, 1, d)
    b = ln_b.astype(F32).reshape(DEPTH, 3, 1, d)
    wq, wkv, att_wo = _attention_weights(att_w_qkv, att_w_o)
    m_w_in, m_w_gates, m_bias, m_wo = _mlstm_weights(mlstm_w_in, mlstm_b_gates, mlstm_w_o)
    sinks = att_sinks.astype(F32)
    for layer in range(DEPTH):
        xt = _ffn_ln(xt, w1, w3, w2, g, b, layer, 0)
        j = layer // 2
        if layer % 2 == 0:
            xt = _attention_mixer(xt, wq, wkv, att_wo, sinks[j], j, g, b, layer, batch, seq)
        else:
            xt = _mlstm_mixer(xt, m_w_in, m_w_gates, m_bias, m_wo, j, g, b, layer, batch, seq)
        xt = _ffn_ln(xt, w1, w3, w2, g, b, layer, 1)
    return xt.reshape(batch, seq, d).astype(x.dtype)
```

```python
import functools

import jax
import jax.numpy as jnp
from jax import lax
from jax.experimental import pallas as pl
from jax.experimental.pallas import tpu as pltpu

D_MODEL = 2048
DEPTH = 4
D_FF = 5632
ATT_HEADS = 32
ATT_KV_HEADS = 4
ATT_GROUP = ATT_HEADS // ATT_KV_HEADS
ATT_HEAD_DIM = D_MODEL // ATT_HEADS
WINDOW = 128
M_HEADS = 4
M_V_DIM = D_MODEL // M_HEADS
M_QK_DIM = M_V_DIM // 2
CHUNK = 128
DEEPNORM_ALPHA = (2.0 * DEPTH) ** 0.25
LN_EPS = 1e-5

LANES = 128
MASKED = -1e30
NO_KEY_DISTANCE = 1e30
VMEM_LIMIT_BYTES = 60 * 1024 * 1024

BF16 = jnp.bfloat16
F32 = jnp.float32


def _compiler_params(semantics):
    return pltpu.CompilerParams(dimension_semantics=semantics,
                                vmem_limit_bytes=VMEM_LIMIT_BYTES)


def _layer_norm(y, g, b):
    mu = jnp.mean(y, axis=-1, keepdims=True)
    yc = y - mu
    var = jnp.mean(yc * yc, axis=-1, keepdims=True)
    return yc * lax.rsqrt(var + LN_EPS) * g + b


LN_ROWS = 16


def _layer_norm_rows(o_ref, first_row, n_rows, g, b, residual=None):
    for c in range(n_rows // LN_ROWS):
        rows = pl.ds(first_row + c * LN_ROWS, LN_ROWS)
        y = o_ref[rows, :]
        if residual is not None:
            y = y + residual(rows)
        o_ref[rows, :] = _layer_norm(y, g, b)


FFN_TM = 1024
FFN_SUB = 512
FFN_TF = 512
FFN_NF = D_FF // FFN_TF
FFN_PREFETCH_F = FFN_NF // 2
assert 0 < FFN_PREFETCH_F < FFN_NF


def _ffn_kernel(x_hbm, w1_ref, w3_ref, w2_ref, g_ref, b_ref, o_ref, xf_ref, xb_ref, sem):
    i = pl.program_id(0)
    f = pl.program_id(1)

    def x_copy(block):
        return pltpu.make_async_copy(x_hbm.at[pl.ds(block * FFN_TM, FFN_TM), :], xf_ref, sem)

    last = pl.num_programs(1) - 1

    def step(is_first, is_last):
        w1 = w1_ref[...].astype(BF16)
        w2 = w2_ref[...].astype(BF16)
        for r in range(FFN_TM // FFN_SUB):
            rows = pl.ds(r * FFN_SUB, FFN_SUB)
            if is_first:
                xf = xf_ref[rows, :]
                xb = xf.astype(BF16)
                xb_ref[rows, :] = xb
            else:
                xb = xb_ref[rows, :]
            h1 = jnp.dot(xb, w1, preferred_element_type=F32)
            h3 = jnp.dot(xb, w3_ref[...], preferred_element_type=F32)
            act = (0.5 * (h1 * jax.nn.sigmoid(h1) * h3)).astype(BF16)
            part = jnp.dot(act, w2, preferred_element_type=F32)
            if is_last and r > 0:
                _layer_norm_rows(o_ref, (r - 1) * FFN_SUB, FFN_SUB, g_ref[...], b_ref[...])
            o_ref[rows, :] = (DEEPNORM_ALPHA * xf if is_first else o_ref[rows, :]) + part
        if is_last:
            _layer_norm_rows(o_ref, FFN_TM - FFN_SUB, FFN_SUB, g_ref[...], b_ref[...])

    @pl.when(f == 0)
    def _():
        @pl.when(i == 0)
        def _():
            x_copy(0).start()

        x_copy(i).wait()
        step(True, False)

    @pl.when(f == FFN_PREFETCH_F)
    def _():
        @pl.when(i + 1 < pl.num_programs(0))
        def _():
            x_copy(i + 1).start()

    @pl.when((f > 0) & (f < last))
    def _():
        step(False, False)

    @pl.when(f == last)
    def _():
        step(False, True)


def _ln_spec(layer, which):
    return pl.BlockSpec((None, None, 1, D_MODEL), lambda *_: (layer, which, 0, 0))


def _ffn_ln(x, w1, w3, w2, g, b, layer, half):
    t = x.shape[0]
    grid = (t // FFN_TM, FFN_NF)
    return pl.pallas_call(
        _ffn_kernel,
        grid=grid,
        in_specs=[
            pl.BlockSpec(memory_space=pl.ANY),
            pl.BlockSpec((None, None, D_MODEL, FFN_TF), lambda i, f: (layer, half, 0, f)),
            pl.BlockSpec((None, None, D_MODEL, FFN_TF), lambda i, f: (layer, half, 0, f)),
            pl.BlockSpec((None, None, FFN_TF, D_MODEL), lambda i, f: (layer, half, f, 0)),
            _ln_spec(layer, 2 * half),
            _ln_spec(layer, 2 * half),
        ],
        out_specs=pl.BlockSpec((FFN_TM, D_MODEL), lambda i, f: (i, 0)),
        out_shape=jax.ShapeDtypeStruct((t, D_MODEL), F32),
        scratch_shapes=[pltpu.VMEM((FFN_TM, D_MODEL), F32),
                        pltpu.VMEM((FFN_TM, D_MODEL), BF16),
                        pltpu.SemaphoreType.DMA(())],
        compiler_params=_compiler_params(("arbitrary", "arbitrary")),
        name="ffn_ln",
    )(x, w1, w3, w2, g, b)


OUT_TM = 512


OUT_SUB = 256


def _outproj_kernel(a_ref, w_ref, x_ref, g_ref, b_ref, o_ref):
    for r in range(OUT_TM // OUT_SUB):
        rows = pl.ds(r * OUT_SUB, OUT_SUB)
        o_ref[rows, :] = jnp.dot(a_ref[rows, :], w_ref[...], preferred_element_type=F32)
        _layer_norm_rows(o_ref, r * OUT_SUB, OUT_SUB, g_ref[...], b_ref[...],
                         residual=lambda ln_rows: DEEPNORM_ALPHA * x_ref[ln_rows, :])


def _outproj_ln(a, w, j, x, g, b, layer):
    t = x.shape[0]
    return pl.pallas_call(
        _outproj_kernel,
        grid=(t // OUT_TM,),
        in_specs=[
            pl.BlockSpec((OUT_TM, D_MODEL), lambda i: (i, 0)),
            pl.BlockSpec((None, D_MODEL, D_MODEL), lambda i: (j, 0, 0)),
            pl.BlockSpec((OUT_TM, D_MODEL), lambda i: (i, 0)),
            _ln_spec(layer, 1),
            _ln_spec(layer, 1),
        ],
        out_specs=pl.BlockSpec((OUT_TM, D_MODEL), lambda i: (i, 0)),
        out_shape=jax.ShapeDtypeStruct((t, D_MODEL), F32),
        compiler_params=_compiler_params(("parallel",)),
        name="outproj_ln",
    )(a, w, x, g, b)


ATT_KV_COLS = ATT_KV_HEADS * ATT_HEAD_DIM
ATT_SUB = 2 * WINDOW
ATT_K_COL0 = D_MODEL
ATT_V_COL0 = D_MODEL + ATT_KV_COLS
ATT_SLOT_COLS = D_MODEL + 2 * ATT_KV_COLS


def _alibi_slope(head):
    return 2.0 ** (-8.0 * (head + 1) / ATT_HEADS)


def _attn_project(x_ref, wq_ref, wkv_ref, qkv_ref, slot):
    xb = x_ref[...].astype(BF16)
    qkv_ref[slot, :, :D_MODEL] = jnp.dot(xb, wq_ref[...], preferred_element_type=F32).astype(BF16)
    qkv_ref[slot, :, D_MODEL:] = jnp.dot(xb, wkv_ref[...], preferred_element_type=F32).astype(BF16)


def _attend_block(sink_ref, q_at, kp_at, kc_at, vp_at, vc_at, no_prev, o_ref, out_rows):
    W = WINDOW
    rows = 2 * W
    qi = lax.broadcasted_iota(jnp.int32, (rows, W), 0) & (W - 1)
    j = lax.broadcasted_iota(jnp.int32, (rows, W), 1)
    from_prev = j > qi
    dist = jnp.where(from_prev, W + qi - j, qi - j).astype(F32)
    dist = jnp.where(from_prev & no_prev, NO_KEY_DISTANCE, dist)
    is_lo_row = lax.broadcasted_iota(jnp.int32, (rows, 1), 0) < W
    lo_lane = lax.broadcasted_iota(jnp.int32, (1, LANES), 1) < ATT_HEAD_DIM
    zero = jnp.zeros((), BF16)

    for c2 in range(ATT_KV_HEADS // 2):
        kk = jnp.concatenate([kp_at(c2), kc_at(c2)], axis=0)
        vv = jnp.concatenate([vp_at(c2), vc_at(c2)], axis=0)
        vcat = jnp.concatenate([jnp.where(lo_lane, vv, zero),
                                jnp.where(lo_lane, zero, vv)], axis=0)
        G = range(ATT_GROUP)
        pair = [c2 * ATT_GROUP + g for g in G]
        head_lo = [(2 * c2) * ATT_GROUP + g for g in G]
        head_hi = [(2 * c2 + 1) * ATT_GROUP + g for g in G]
        slope = [jnp.where(is_lo_row, _alibi_slope(head_lo[g]), _alibi_slope(head_hi[g])) for g in G]
        sink = [jnp.where(is_lo_row, sink_ref[head_lo[g]], sink_ref[head_hi[g]]) for g in G]
        qp = [q_at(pair[g]) * jnp.asarray(ATT_HEAD_DIM ** -0.5, BF16) for g in G]
        qs = [jnp.concatenate([jnp.where(lo_lane, qp[g], zero),
                               jnp.where(lo_lane, zero, qp[g])], axis=0) for g in G]
        s2_all = lax.dot_general(jnp.concatenate(qs, axis=0), kk, (((1,), (1,)), ((), ())),
                                 preferred_element_type=F32)
        s2 = [s2_all[g * rows:(g + 1) * rows] for g in G]
        s = [jnp.where(from_prev, s2[g][:, :W], s2[g][:, W:]) - slope[g] * dist for g in G]
        m = [jnp.maximum(jnp.max(s[g], axis=-1, keepdims=True), sink[g]) for g in G]
        p = [jnp.exp(s[g] - m[g]) for g in G]
        denom = [jnp.sum(p[g], axis=-1, keepdims=True) + jnp.exp(sink[g] - m[g]) for g in G]
        p = [p[g] * (1.0 / denom[g]) for g in G]
        pcat = []
        for g in G:
            p_prev = jnp.where(from_prev, p[g], 0.0).astype(BF16)
            p_cur = jnp.where(from_prev, 0.0, p[g]).astype(BF16)
            pcat.append(jnp.concatenate([p_prev[:W], p_cur[:W], p_prev[W:], p_cur[W:]], axis=1))
        o_all = jnp.dot(jnp.concatenate(pcat, axis=0), vcat, preferred_element_type=F32)
        for g in G:
            o_ref[out_rows, LANES * pair[g]:LANES * (pair[g] + 1)] = o_all[g * W:(g + 1) * W].astype(BF16)


def _attn_kernel(sink_ref, xa_ref, xb_ref, wq_ref, wkv_ref, o_ref, qkv_ref, carry_ref):
    s = pl.program_id(1)
    last = pl.num_programs(1) - 1
    W = WINDOW

    def project(x_ref, slot):
        _attn_project(x_ref, wq_ref, wkv_ref, qkv_ref, slot)

    def attend(slot, out_row0, no_prev):
        lanes = lambda c2: slice(LANES * c2, LANES * (c2 + 1))
        first, second = slice(0, W), slice(W, 2 * W)
        slot_at = lambda rows, col0: (
            lambda c2: qkv_ref[slot, rows, col0 + LANES * c2:col0 + LANES * (c2 + 1)])
        _attend_block(sink_ref,
                      lambda pair: qkv_ref[slot, first, lanes(pair)],
                      lambda c2: carry_ref[:, lanes(c2)],
                      slot_at(first, ATT_K_COL0),
                      lambda c2: carry_ref[:, ATT_KV_COLS + LANES * c2:ATT_KV_COLS + LANES * (c2 + 1)],
                      slot_at(first, ATT_V_COL0),
                      no_prev, o_ref, slice(out_row0, out_row0 + W))
        _attend_block(sink_ref,
                      lambda pair: qkv_ref[slot, second, lanes(pair)],
                      slot_at(first, ATT_K_COL0), slot_at(second, ATT_K_COL0),
                      slot_at(first, ATT_V_COL0), slot_at(second, ATT_V_COL0),
                      False, o_ref, slice(out_row0 + W, out_row0 + 2 * W))
        carry_ref[...] = qkv_ref[slot, second, ATT_K_COL0:]

    @pl.when(s == 0)
    def _():
        carry_ref[...] = jnp.zeros_like(carry_ref)
        project(xb_ref, 0)

    @pl.when((s > 0) & (s < last))
    def _():
        project(xa_ref, 1)
        attend(0, 0, s == 1)
        project(xb_ref, 0)
        attend(1, ATT_SUB, False)

    @pl.when(s == last)
    def _():
        project(xa_ref, 1)
        attend(0, 0, s == 1)
        attend(1, ATT_SUB, False)


def _attn_proj_core(sinks, x, wq, wkv, j, batch, seq):
    nu = seq // ATT_SUB
    assert nu % 2 == 0
    const = lambda b, s: (j, 0, 0)
    return pl.pallas_call(
        _attn_kernel,
        grid=(batch, nu // 2 + 1),
        in_specs=[
            pl.BlockSpec(memory_space=pltpu.SMEM),
            pl.BlockSpec((None, ATT_SUB, D_MODEL), lambda b, s: (b, jnp.maximum(2 * s - 1, 0), 0)),
            pl.BlockSpec((None, ATT_SUB, D_MODEL), lambda b, s: (b, jnp.minimum(2 * s, nu - 1), 0)),
            pl.BlockSpec((None, D_MODEL, D_MODEL), const, pipeline_mode=pl.Buffered(1)),
            pl.BlockSpec((None, D_MODEL, 2 * ATT_KV_COLS), const, pipeline_mode=pl.Buffered(1)),
        ],
        out_specs=pl.BlockSpec((None, 2 * ATT_SUB, D_MODEL), lambda b, s: (b, jnp.maximum(s - 1, 0), 0)),
        out_shape=jax.ShapeDtypeStruct((batch, seq, D_MODEL), BF16),
        scratch_shapes=[
            pltpu.VMEM((2, ATT_SUB, ATT_SLOT_COLS), BF16),
            pltpu.VMEM((WINDOW, 2 * ATT_KV_COLS), BF16),
        ],
        compiler_params=_compiler_params(("parallel", "arbitrary")),
        name="swa_proj_core",
    )(sinks, x, x, wq, wkv)


def _to_head_pair_order(w, axis):
    shape = w.shape
    split = shape[:axis] + (ATT_KV_HEADS // 2, 2, ATT_GROUP, ATT_HEAD_DIM) + shape[axis + 1:]
    order = list(range(len(split)))
    order[axis + 1], order[axis + 2] = order[axis + 2], order[axis + 1]
    return w.reshape(split).transpose(order).reshape(shape)


def _attention_weights(att_w_qkv, att_w_o):
    nq = ATT_HEADS * ATT_HEAD_DIM
    wq = _to_head_pair_order(att_w_qkv[:, :, :nq], 2).astype(BF16)
    wkv = att_w_qkv[:, :, nq:].astype(BF16)
    wo = _to_head_pair_order(att_w_o, 1).astype(BF16)
    return wq, wkv, wo


def _attention_mixer(x, wq, wkv, wo, sinks, j, g, b, layer, batch, seq):
    o = _attn_proj_core(sinks, x.reshape(batch, seq, D_MODEL), wq, wkv, j, batch, seq)
    return _outproj_ln(o.reshape(batch * seq, D_MODEL), wo, j, x, g, b, layer)


MPROJ_TN = 1536
M_MAIN_COLS = 2 * M_HEADS * M_QK_DIM + 2 * M_HEADS * M_V_DIM
M_GATE_COLS = 2 * M_HEADS
M_K_COL0 = M_HEADS * M_QK_DIM
M_V_COL0 = 2 * M_HEADS * M_QK_DIM
M_OG_COL0 = M_V_COL0 + M_HEADS * M_V_DIM
M_BATCH = 2


def _log_sigmoid(x):
    return jnp.minimum(x, 0.0) - jnp.log1p(jnp.exp(-jnp.abs(x)))


def _mlstm_project(x_ref, w_ref, wg_ref, bg_ref, proj_ref, gbuf_ref, slot):
    xb = jnp.concatenate([x_ref[bi] for bi in range(M_BATCH)], axis=0).astype(BF16)
    for nt in range(M_MAIN_COLS // MPROJ_TN):
        cols = slice(nt * MPROJ_TN, (nt + 1) * MPROJ_TN)
        proj_ref[slot, :, cols] = jnp.dot(xb, w_ref[:, cols], preferred_element_type=F32).astype(BF16)
    gbuf_ref[slot] = jnp.dot(xb, wg_ref[...], preferred_element_type=F32) + bg_ref[...]


def _mlstm_chunk(proj_ref, gbuf_ref, slot, o_ref, out_row0, c_ref, n_ref, m_ref):
    L = CHUNK
    t_idx = lax.broadcasted_iota(jnp.int32, (L, L), 0)
    s_idx = lax.broadcasted_iota(jnp.int32, (L, L), 1)
    causal = s_idx <= t_idx

    units = [(bi, h) for bi in range(M_BATCH) for h in range(M_HEADS)]
    rows = [slice(bi * L, (bi + 1) * L) for bi, _ in units]
    out_rows = slice(out_row0, out_row0 + L)
    qk_cols = [slice(h * M_QK_DIM, (h + 1) * M_QK_DIM) for _, h in units]
    k_cols = [slice(M_K_COL0 + h * M_QK_DIM, M_K_COL0 + (h + 1) * M_QK_DIM) for _, h in units]
    v_cols = [slice(h * M_V_DIM, (h + 1) * M_V_DIM) for _, h in units]
    pv_cols = [slice(M_V_COL0 + h * M_V_DIM, M_V_COL0 + (h + 1) * M_V_DIM) for _, h in units]
    og_cols = [slice(M_OG_COL0 + h * M_V_DIM, M_OG_COL0 + (h + 1) * M_V_DIM) for _, h in units]
    gates = [gbuf_ref[slot, bi * L:(bi + 1) * L, :] for bi in range(M_BATCH)]
    gates_t = [gt.T for gt in gates]

    ig_col, b_col, dmat, row_max = [], [], [], []
    for bi, h in units:
        ig_col.append(gates[bi][:, h:h + 1])
        ig_row = gates_t[bi][h:h + 1, :]
        lf_col = _log_sigmoid(gates[bi][:, M_HEADS + h:M_HEADS + h + 1])
        lf_row = _log_sigmoid(gates_t[bi][M_HEADS + h:M_HEADS + h + 1, :])
        bc = jnp.sum(jnp.where(causal, lf_row, 0.0), axis=1, keepdims=True)
        br = jnp.sum(jnp.where(t_idx <= s_idx, lf_col, 0.0), axis=0, keepdims=True)
        d = jnp.where(causal, bc - br + ig_row, MASKED)
        b_col.append(bc)
        dmat.append(d)
        row_max.append(jnp.max(d, axis=1, keepdims=True))

    q = [proj_ref[slot, rows[u], qk_cols[u]] for u in range(len(units))]
    k = [proj_ref[slot, rows[u], k_cols[u]] * jnp.asarray(M_QK_DIM ** -0.5, BF16)
         for u in range(len(units))]
    qk = [lax.dot_general(q[u], k[u], (((1,), (1,)), ((), ())), preferred_element_type=F32)
          for u in range(len(units))]
    qc = [jnp.dot(q[u], c_ref[bi, h].astype(BF16), preferred_element_type=F32)
          for u, (bi, h) in enumerate(units)]

    m_prev = [m_ref[bi, h][:, 0:1] for bi, h in units]
    m_rows = [jnp.maximum(b_col[u] + m_prev[u], row_max[u]) for u in range(len(units))]
    inter = [jnp.exp(b_col[u] + m_prev[u] - m_rows[u]) for u in range(len(units))]
    w = [qk[u] * jnp.exp(dmat[u] - m_rows[u]) for u in range(len(units))]

    for u, (bi, h) in enumerate(units):
        v = proj_ref[slot, rows[u], pv_cols[u]]
        num = inter[u] * qc[u] + jnp.dot(w[u].astype(BF16), v, preferred_element_type=F32)
        qn = jnp.sum(q[u].astype(F32) * n_ref[bi, h], axis=1, keepdims=True)
        den = inter[u] * qn + jnp.sum(w[u], axis=1, keepdims=True)
        hid = num * (1.0 / jnp.maximum(jnp.abs(den), jnp.exp(-m_rows[u])))
        og = proj_ref[slot, rows[u], og_cols[u]].astype(F32)
        o_ref[bi, out_rows, v_cols[u]] = (hid * jax.nn.sigmoid(og)).astype(BF16)

    for u, (bi, h) in enumerate(units):
        v = proj_ref[slot, rows[u], pv_cols[u]]
        m_end = m_rows[u][L - 1:L, :]
        b_end = b_col[u][L - 1:L, :]
        decay = jnp.exp(b_end + m_prev[u] - m_end)
        wk = jnp.exp(b_end - b_col[u] + ig_col[u] - m_end)
        kw = k[u].astype(F32) * wk
        c_ref[bi, h] = decay * c_ref[bi, h] + lax.dot_general(
            kw.astype(BF16), v, (((0,), (0,)), ((), ())), preferred_element_type=F32)
        n_ref[bi, h] = decay * n_ref[bi, h] + jnp.sum(kw, axis=0, keepdims=True)
        m_ref[bi, h] = jnp.broadcast_to(m_end, (1, LANES))


def _mlstm_kernel(xa_ref, xb_ref, w_ref, wg_ref, bg_ref, o_ref,
                  proj_ref, gbuf_ref, c_ref, n_ref, m_ref):
    s = pl.program_id(1)
    last = pl.num_programs(1) - 1
    L = CHUNK

    def project(x_ref, slot):
        _mlstm_project(x_ref, w_ref, wg_ref, bg_ref, proj_ref, gbuf_ref, slot)

    def chunk(slot, out_row0):
        _mlstm_chunk(proj_ref, gbuf_ref, slot, o_ref, out_row0, c_ref, n_ref, m_ref)

    @pl.when(s == 0)
    def _():
        c_ref[...] = jnp.zeros_like(c_ref)
        n_ref[...] = jnp.zeros_like(n_ref)
        m_ref[...] = jnp.zeros_like(m_ref)
        project(xb_ref, 0)

    @pl.when((s > 0) & (s < last))
    def _():
        project(xa_ref, 1)
        chunk(0, 0)
        project(xb_ref, 0)
        chunk(1, L)

    @pl.when(s == last)
    def _():
        project(xa_ref, 1)
        chunk(0, 0)
        chunk(1, L)


def _mlstm_proj_core(x, w_in, w_gates, b_gates, j, batch, seq):
    nc = seq // CHUNK
    v_cols = M_HEADS * M_V_DIM
    assert batch % M_BATCH == 0 and nc % 2 == 0
    const = lambda b, s: (j, 0, 0)
    return pl.pallas_call(
        _mlstm_kernel,
        grid=(batch // M_BATCH, nc // 2 + 1),
        in_specs=[
            pl.BlockSpec((M_BATCH, CHUNK, D_MODEL), lambda b, s: (b, jnp.maximum(2 * s - 1, 0), 0)),
            pl.BlockSpec((M_BATCH, CHUNK, D_MODEL), lambda b, s: (b, jnp.minimum(2 * s, nc - 1), 0)),
            pl.BlockSpec((None, D_MODEL, M_MAIN_COLS), const, pipeline_mode=pl.Buffered(1)),
            pl.BlockSpec((None, D_MODEL, LANES), const),
            pl.BlockSpec((None, 1, LANES), const),
        ],
        out_specs=pl.BlockSpec((M_BATCH, 2 * CHUNK, v_cols), lambda b, s: (b, jnp.maximum(s - 1, 0), 0)),
        out_shape=jax.ShapeDtypeStruct((batch, seq, v_cols), BF16),
        scratch_shapes=[
            pltpu.VMEM((2, M_BATCH * CHUNK, M_MAIN_COLS), BF16),
            pltpu.VMEM((2, M_BATCH * CHUNK, LANES), F32),
            pltpu.VMEM((M_BATCH, M_HEADS, M_QK_DIM, M_V_DIM), F32),
            pltpu.VMEM((M_BATCH, M_HEADS, 1, M_QK_DIM), F32),
            pltpu.VMEM((M_BATCH, M_HEADS, 1, LANES), F32),
        ],
        compiler_params=_compiler_params(("parallel", "arbitrary")),
        name="mlstm_proj_core",
    )(x, x, w_in, w_gates, b_gates)


def _mlstm_weights(mlstm_w_in, mlstm_b_gates, mlstm_w_o):
    pad = LANES - M_GATE_COLS
    w_in = mlstm_w_in[:, :, :M_MAIN_COLS].astype(BF16)
    w_gates = jnp.pad(mlstm_w_in[:, :, M_MAIN_COLS:], ((0, 0), (0, 0), (0, pad))).astype(BF16)
    bias = jnp.pad(mlstm_b_gates.astype(F32), ((0, 0), (0, pad)))[:, None, :]
    return w_in, w_gates, bias, mlstm_w_o.astype(BF16)


def _mlstm_mixer(x, w_in, w_gates, bias, w_o, j, g, b, layer, batch, seq):
    hid = _mlstm_proj_core(x.reshape(batch, seq, D_MODEL), w_in, w_gates, bias, j, batch, seq)
    return _outproj_ln(hid.reshape(batch * seq, D_MODEL), w_o, j, x, g, b, layer)


def kernel(x, ffn_w1, ffn_w3, ffn_w2, ln_g, ln_b, att_w_qkv, att_sinks, att_w_o,
           mlstm_w_in, mlstm_b_gates, mlstm_w_o):
    batch, seq, d = x.shape
    assert d == D_MODEL and seq % WINDOW == 0 and (batch * seq) % FFN_TM == 0
    xt = x.reshape(batch * seq, d)
    w1 = ffn_w1.astype(F32)
    w3 = ffn_w3.astype(BF16)
    w2 = ffn_w2.astype(F32)
    g = ln_g.astype(F32).reshape(DEPTH, 3, 1, d)
    b = ln_b.astype(F32).reshape(DEPTH, 3, 1, d)
    wq, wkv, att_wo = _attention_weights(att_w_qkv, att_w_o)
    m_w_in, m_w_gates, m_bias, m_wo = _mlstm_weights(mlstm_w_in, mlstm_b_gates, mlstm_w_o)
    sinks = att_sinks.astype(F32)
    for layer in range(DEPTH):
        xt = _ffn_ln(xt, w1, w3, w2, g, b, layer, 0)
        j = layer // 2
        if layer % 2 == 0:
            xt = _attention_mixer(xt, wq, wkv, att_wo, sinks[j], j, g, b, layer, batch, seq)
        else:
            xt = _mlstm_mixer(xt, m_w_in, m_w_gates, m_bias, m_wo, j, g, b, layer, batch, seq)
        xt = _ffn_ln(xt, w1, w3, w2, g, b, layer, 1)
    return xt.reshape(batch, seq, d).astype(x.dtype)
```

```python
import functools

import jax
import jax.numpy as jnp
from jax import lax
from jax.experimental import pallas as pl
from jax.experimental.pallas import tpu as pltpu

D_MODEL = 2048
DEPTH = 4
D_FF = 5632
ATT_HEADS = 32
ATT_KV_HEADS = 4
ATT_GROUP = ATT_HEADS // ATT_KV_HEADS
ATT_HEAD_DIM = D_MODEL // ATT_HEADS
WINDOW = 128
M_HEADS = 4
M_V_DIM = D_MODEL // M_HEADS
M_QK_DIM = M_V_DIM // 2
CHUNK = 128
DEEPNORM_ALPHA = (2.0 * DEPTH) ** 0.25
LN_EPS = 1e-5

LANES = 128
MASKED = -1e30
NO_KEY_DISTANCE = 1e30
VMEM_LIMIT_BYTES = 60 * 1024 * 1024

BF16 = jnp.bfloat16
F32 = jnp.float32


def _compiler_params(semantics):
    return pltpu.CompilerParams(dimension_semantics=semantics,
                                vmem_limit_bytes=VMEM_LIMIT_BYTES)


def _layer_norm(y, g, b):
    mu = jnp.mean(y, axis=-1, keepdims=True)
    yc = y - mu
    var = jnp.mean(yc * yc, axis=-1, keepdims=True)
    return yc * lax.rsqrt(var + LN_EPS) * g + b


LN_ROWS = 16


def _layer_norm_rows(o_ref, first_row, n_rows, g, b, residual=None):
    for c in range(n_rows // LN_ROWS):
        rows = pl.ds(first_row + c * LN_ROWS, LN_ROWS)
        y = o_ref[rows, :]
        if residual is not None:
            y = y + residual(rows)
        o_ref[rows, :] = _layer_norm(y, g, b)


FFN_TM = 1024
FFN_SUB = 512
FFN_TF = 512
FFN_NF = D_FF // FFN_TF
FFN_PREFETCH_F = FFN_NF // 2
assert 0 < FFN_PREFETCH_F < FFN_NF


def _ffn_kernel(x_hbm, w1_ref, w3_ref, w2_ref, g_ref, b_ref, o_ref, xf_ref, xb_ref, sem):
    i = pl.program_id(0)
    f = pl.program_id(1)

    def x_copy(block):
        return pltpu.make_async_copy(x_hbm.at[pl.ds(block * FFN_TM, FFN_TM), :], xf_ref, sem)

    last = pl.num_programs(1) - 1

    def step(is_first, is_last):
        w1 = w1_ref[...].astype(BF16)
        w2 = w2_ref[...].astype(BF16)
        for r in range(FFN_TM // FFN_SUB):
            rows = pl.ds(r * FFN_SUB, FFN_SUB)
            if is_first:
                xf = xf_ref[rows, :]
                xb = xf.astype(BF16)
                xb_ref[rows, :] = xb
            else:
                xb = xb_ref[rows, :]
            h1 = jnp.dot(xb, w1, preferred_element_type=F32)
            h3 = jnp.dot(xb, w3_ref[...], preferred_element_type=F32)
            act = (0.5 * (h1 * jax.nn.sigmoid(h1) * h3)).astype(BF16)
            part = jnp.dot(act, w2, preferred_element_type=F32)
            if is_last and r > 0:
                _layer_norm_rows(o_ref, (r - 1) * FFN_SUB, FFN_SUB, g_ref[...], b_ref[...])
            o_ref[rows, :] = (DEEPNORM_ALPHA * xf if is_first else o_ref[rows, :]) + part
        if is_last:
            _layer_norm_rows(o_ref, FFN_TM - FFN_SUB, FFN_SUB, g_ref[...], b_ref[...])

    @pl.when(f == 0)
    def _():
        @pl.when(i == 0)
        def _():
            x_copy(0).start()

        x_copy(i).wait()
        step(True, False)

    @pl.when(f == FFN_PREFETCH_F)
    def _():
        @pl.when(i + 1 < pl.num_programs(0))
        def _():
            x_copy(i + 1).start()

    @pl.when((f > 0) & (f < last))
    def _():
        step(False, False)

    @pl.when(f == last)
    def _():
        step(False, True)


def _ln_spec(layer, which):
    return pl.BlockSpec((None, None, 1, D_MODEL), lambda *_: (layer, which, 0, 0))


def _ffn_ln(x, w1, w3, w2, g, b, layer, half):
    t = x.shape[0]
    grid = (t // FFN_TM, FFN_NF)
    return pl.pallas_call(
        _ffn_kernel,
        grid=grid,
        in_specs=[
            pl.BlockSpec(memory_space=pl.ANY),
            pl.BlockSpec((None, None, D_MODEL, FFN_TF), lambda i, f: (layer, half, 0, f)),
            pl.BlockSpec((None, None, D_MODEL, FFN_TF), lambda i, f: (layer, half, 0, f)),
            pl.BlockSpec((None, None, FFN_TF, D_MODEL), lambda i, f: (layer, half, f, 0)),
            _ln_spec(layer, 2 * half),
            _ln_spec(layer, 2 * half),
        ],
        out_specs=pl.BlockSpec((FFN_TM, D_MODEL), lambda i, f: (i, 0)),
        out_shape=jax.ShapeDtypeStruct((t, D_MODEL), F32),
        scratch_shapes=[pltpu.VMEM((FFN_TM, D_MODEL), F32),
                        pltpu.VMEM((FFN_TM, D_MODEL), BF16),
                        pltpu.SemaphoreType.DMA(())],
        compiler_params=_compiler_params(("arbitrary", "arbitrary")),
        name="ffn_ln",
    )(x, w1, w3, w2, g, b)


OUT_TM = 512
OUT_SUB = 256


def _outproj_kernel(a_ref, w_ref, x_ref, g_ref, b_ref, o_ref):
    for r in range(OUT_TM // OUT_SUB):
        rows = pl.ds(r * OUT_SUB, OUT_SUB)
        o_ref[rows, :] = jnp.dot(a_ref[rows, :], w_ref[...], preferred_element_type=F32)
        _layer_norm_rows(o_ref, r * OUT_SUB, OUT_SUB, g_ref[...], b_ref[...],
                         residual=lambda ln_rows: DEEPNORM_ALPHA * x_ref[ln_rows, :])


def _outproj_ln(a, w, j, x, g, b, layer):
    t = x.shape[0]
    return pl.pallas_call(
        _outproj_kernel,
        grid=(t // OUT_TM,),
        in_specs=[
            pl.BlockSpec((OUT_TM, D_MODEL), lambda i: (i, 0)),
            pl.BlockSpec((None, D_MODEL, D_MODEL), lambda i: (j, 0, 0)),
            pl.BlockSpec((OUT_TM, D_MODEL), lambda i: (i, 0)),
            _ln_spec(layer, 1),
            _ln_spec(layer, 1),
        ],
        out_specs=pl.BlockSpec((OUT_TM, D_MODEL), lambda i: (i, 0)),
        out_shape=jax.ShapeDtypeStruct((t, D_MODEL), F32),
        compiler_params=_compiler_params(("parallel",)),
        name="outproj_ln",
    )(a, w, x, g, b)


ATT_KV_COLS = ATT_KV_HEADS * ATT_HEAD_DIM
ATT_SUB = 2 * WINDOW
ATT_K_COL0 = D_MODEL
ATT_V_COL0 = D_MODEL + ATT_KV_COLS
ATT_SLOT_COLS = D_MODEL + 2 * ATT_KV_COLS


def _alibi_slope(head):
    return 2.0 ** (-8.0 * (head + 1) / ATT_HEADS)


def _attn_project(x_ref, wq_ref, wkv_ref, qkv_ref, slot):
    xb = x_ref[...].astype(BF16)
    qkv_ref[slot, :, :D_MODEL] = jnp.dot(xb, wq_ref[...], preferred_element_type=F32).astype(BF16)
    qkv_ref[slot, :, D_MODEL:] = jnp.dot(xb, wkv_ref[...], preferred_element_type=F32).astype(BF16)


def _attend_block(sink_ref, q_at, kp_at, kc_at, vp_at, vc_at, no_prev, o_ref, out_rows):
    W = WINDOW
    rows = 2 * W
    qi = lax.broadcasted_iota(jnp.int32, (rows, W), 0) & (W - 1)
    j = lax.broadcasted_iota(jnp.int32, (rows, W), 1)
    from_prev = j > qi
    dist = jnp.where(from_prev, W + qi - j, qi - j).astype(F32)
    dist = jnp.where(from_prev & no_prev, NO_KEY_DISTANCE, dist)
    is_lo_row = lax.broadcasted_iota(jnp.int32, (rows, 1), 0) < W
    lo_lane = lax.broadcasted_iota(jnp.int32, (1, LANES), 1) < ATT_HEAD_DIM
    zero = jnp.zeros((), BF16)

    for c2 in range(ATT_KV_HEADS // 2):
        kk = jnp.concatenate([kp_at(c2), kc_at(c2)], axis=0)
        vv = jnp.concatenate([vp_at(c2), vc_at(c2)], axis=0)
        vcat = jnp.concatenate([jnp.where(lo_lane, vv, zero),
                                jnp.where(lo_lane, zero, vv)], axis=0)
        G = range(ATT_GROUP)
        pair = [c2 * ATT_GROUP + g for g in G]
        head_lo = [(2 * c2) * ATT_GROUP + g for g in G]
        head_hi = [(2 * c2 + 1) * ATT_GROUP + g for g in G]
        slope = [jnp.where(is_lo_row, _alibi_slope(head_lo[g]), _alibi_slope(head_hi[g])) for g in G]
        sink = [jnp.where(is_lo_row, sink_ref[head_lo[g]], sink_ref[head_hi[g]]) for g in G]
        qp = [q_at(pair[g]) * jnp.asarray(ATT_HEAD_DIM ** -0.5, BF16) for g in G]
        qs = [jnp.concatenate([jnp.where(lo_lane, qp[g], zero),
                               jnp.where(lo_lane, zero, qp[g])], axis=0) for g in G]
        s2_all = lax.dot_general(jnp.concatenate(qs, axis=0), kk, (((1,), (1,)), ((), ())),
                                 preferred_element_type=F32)
        s2 = [s2_all[g * rows:(g + 1) * rows] for g in G]
        s = [jnp.where(from_prev, s2[g][:, :W], s2[g][:, W:]) - slope[g] * dist for g in G]
        m = [jnp.maximum(jnp.max(s[g], axis=-1, keepdims=True), sink[g]) for g in G]
        p = [jnp.exp(s[g] - m[g]) for g in G]
        denom = [jnp.sum(p[g], axis=-1, keepdims=True) + jnp.exp(sink[g] - m[g]) for g in G]
        p = [p[g] * (1.0 / denom[g]) for g in G]
        pcat = []
        for g in G:
            p_prev = jnp.where(from_prev, p[g], 0.0).astype(BF16)
            p_cur = jnp.where(from_prev, 0.0, p[g]).astype(BF16)
            pcat.append(jnp.concatenate([p_prev[:W], p_cur[:W], p_prev[W:], p_cur[W:]], axis=1))
        o_all = jnp.dot(jnp.concatenate(pcat, axis=0), vcat, preferred_element_type=F32)
        for g in G:
            o_ref[out_rows, LANES * pair[g]:LANES * (pair[g] + 1)] = o_all[g * W:(g + 1) * W].astype(BF16)


def _attn_kernel(sink_ref, xa_ref, xb_ref, wq_ref, wkv_ref, o_ref, qkv_ref, carry_ref):
    s = pl.program_id(1)
    last = pl.num_programs(1) - 1
    W = WINDOW

    def project(x_ref, slot):
        _attn_project(x_ref, wq_ref, wkv_ref, qkv_ref, slot)

    def attend(slot, out_row0, no_prev):
        lanes = lambda c2: slice(LANES * c2, LANES * (c2 + 1))
        first, second = slice(0, W), slice(W, 2 * W)
        slot_at = lambda rows, col0: (
            lambda c2: qkv_ref[slot, rows, col0 + LANES * c2:col0 + LANES * (c2 + 1)])
        _attend_block(sink_ref,
                      lambda pair: qkv_ref[slot, first, lanes(pair)],
                      lambda c2: carry_ref[:, lanes(c2)],
                      slot_at(first, ATT_K_COL0),
                      lambda c2: carry_ref[:, ATT_KV_COLS + LANES * c2:ATT_KV_COLS + LANES * (c2 + 1)],
                      slot_at(first, ATT_V_COL0),
                      no_prev, o_ref, slice(out_row0, out_row0 + W))
        _attend_block(sink_ref,
                      lambda pair: qkv_ref[slot, second, lanes(pair)],
                      slot_at(first, ATT_K_COL0), slot_at(second, ATT_K_COL0),
                      slot_at(first, ATT_V_COL0), slot_at(second, ATT_V_COL0),
                      False, o_ref, slice(out_row0 + W, out_row0 + 2 * W))
        carry_ref[...] = qkv_ref[slot, second, ATT_K_COL0:]

    @pl.when(s == 0)
    def _():
        carry_ref[...] = jnp.zeros_like(carry_ref)
        project(xb_ref, 0)

    @pl.when((s > 0) & (s < last))
    def _():
        project(xa_ref, 1)
        attend(0, 0, s == 1)
        project(xb_ref, 0)
        attend(1, ATT_SUB, False)

    @pl.when(s == last)
    def _():
        project(xa_ref, 1)
        attend(0, 0, s == 1)
        attend(1, ATT_SUB, False)


def _attn_proj_core(sinks, x, wq, wkv, j, batch, seq):
    nu = seq // ATT_SUB
    assert nu % 2 == 0
    const = lambda b, s: (j, 0, 0)
    return pl.pallas_call(
        _attn_kernel,
        grid=(batch, nu // 2 + 1),
        in_specs=[
            pl.BlockSpec(memory_space=pltpu.SMEM),
            pl.BlockSpec((None, ATT_SUB, D_MODEL), lambda b, s: (b, jnp.maximum(2 * s - 1, 0), 0)),
            pl.BlockSpec((None, ATT_SUB, D_MODEL), lambda b, s: (b, jnp.minimum(2 * s, nu - 1), 0)),
            pl.BlockSpec((None, D_MODEL, D_MODEL), const, pipeline_mode=pl.Buffered(1)),
            pl.BlockSpec((None, D_MODEL, 2 * ATT_KV_COLS), const, pipeline_mode=pl.Buffered(1)),
        ],
        out_specs=pl.BlockSpec((None, 2 * ATT_SUB, D_MODEL), lambda b, s: (b, jnp.maximum(s - 1, 0), 0)),
        out_shape=jax.ShapeDtypeStruct((batch, seq, D_MODEL), BF16),
        scratch_shapes=[
            pltpu.VMEM((2, ATT_SUB, ATT_SLOT_COLS), BF16),
            pltpu.VMEM((WINDOW, 2 * ATT_KV_COLS), BF16),
        ],
        compiler_params=_compiler_params(("parallel", "arbitrary")),
        name="swa_proj_core",
    )(sinks, x, x, wq, wkv)


def _to_head_pair_order(w, axis):
    shape = w.shape
    split = shape[:axis] + (ATT_KV_HEADS // 2, 2, ATT_GROUP, ATT_HEAD_DIM) + shape[axis + 1:]
    order = list(range(len(split)))
    order[axis + 1], order[axis + 2] = order[axis + 2], order[axis + 1]
    return w.reshape(split).transpose(order).reshape(shape)


def _attention_weights(att_w_qkv, att_w_o):
    nq = ATT_HEADS * ATT_HEAD_DIM
    wq = _to_head_pair_order(att_w_qkv[:, :, :nq], 2).astype(BF16)
    wkv = att_w_qkv[:, :, nq:].astype(BF16)
    wo = _to_head_pair_order(att_w_o, 1).astype(BF16)
    return wq, wkv, wo


def _attention_mixer(x, wq, wkv, wo, sinks, j, g, b, layer, batch, seq):
    o = _attn_proj_core(sinks, x.reshape(batch, seq, D_MODEL), wq, wkv, j, batch, seq)
    return _outproj_ln(o.reshape(batch * seq, D_MODEL), wo, j, x, g, b, layer)


MPROJ_TN = 1536
M_MAIN_COLS = 2 * M_HEADS * M_QK_DIM + 2 * M_HEADS * M_V_DIM
M_GATE_COLS = 2 * M_HEADS
M_K_COL0 = M_HEADS * M_QK_DIM
M_V_COL0 = 2 * M_HEADS * M_QK_DIM
M_OG_COL0 = M_V_COL0 + M_HEADS * M_V_DIM
M_BATCH = 2


def _log_sigmoid(x):
    return jnp.minimum(x, 0.0) - jnp.log1p(jnp.exp(-jnp.abs(x)))


def _mlstm_project(x_ref, w_ref, wg_ref, bg_ref, proj_ref, gbuf_ref, slot):
    xb = jnp.concatenate([x_ref[bi] for bi in range(M_BATCH)], axis=0).astype(BF16)
    for nt in range(M_MAIN_COLS // MPROJ_TN):
        cols = slice(nt * MPROJ_TN, (nt + 1) * MPROJ_TN)
        proj_ref[slot, :, cols] = jnp.dot(xb, w_ref[:, cols], preferred_element_type=F32).astype(BF16)
    gbuf_ref[slot] = jnp.dot(xb, wg_ref[...], preferred_element_type=F32) + bg_ref[...]


def _mlstm_chunk(proj_ref, gbuf_ref, slot, o_ref, out_row0, c_ref, n_ref, m_ref):
    L = CHUNK
    t_idx = lax.broadcasted_iota(jnp.int32, (L, L), 0)
    s_idx = lax.broadcasted_iota(jnp.int32, (L, L), 1)
    causal = s_idx <= t_idx

    units = [(bi, h) for bi in range(M_BATCH) for h in range(M_HEADS)]
    rows = [slice(bi * L, (bi + 1) * L) for bi, _ in units]
    out_rows = slice(out_row0, out_row0 + L)
    qk_cols = [slice(h * M_QK_DIM, (h + 1) * M_QK_DIM) for _, h in units]
    k_cols = [slice(M_K_COL0 + h * M_QK_DIM, M_K_COL0 + (h + 1) * M_QK_DIM) for _, h in units]
    v_cols = [slice(h * M_V_DIM, (h + 1) * M_V_DIM) for _, h in units]
    pv_cols = [slice(M_V_COL0 + h * M_V_DIM, M_V_COL0 + (h + 1) * M_V_DIM) for _, h in units]
    og_cols = [slice(M_OG_COL0 + h * M_V_DIM, M_OG_COL0 + (h + 1) * M_V_DIM) for _, h in units]
    gates = [gbuf_ref[slot, bi * L:(bi + 1) * L, :] for bi in range(M_BATCH)]
    gates_t = [gt.T for gt in gates]

    ig_col, b_col, dmat, row_max = [], [], [], []
    for bi, h in units:
        ig_col.append(gates[bi][:, h:h + 1])
        ig_row = gates_t[bi][h:h + 1, :]
        lf_col = _log_sigmoid(gates[bi][:, M_HEADS + h:M_HEADS + h + 1])
        lf_row = _log_sigmoid(gates_t[bi][M_HEADS + h:M_HEADS + h + 1, :])
        bc = jnp.sum(jnp.where(causal, lf_row, 0.0), axis=1, keepdims=True)
        br = jnp.sum(jnp.where(t_idx <= s_idx, lf_col, 0.0), axis=0, keepdims=True)
        d = jnp.where(causal, bc - br + ig_row, MASKED)
        b_col.append(bc)
        dmat.append(d)
        row_max.append(jnp.max(d, axis=1, keepdims=True))

    q = [proj_ref[slot, rows[u], qk_cols[u]] for u in range(len(units))]
    k = [proj_ref[slot, rows[u], k_cols[u]] * jnp.asarray(M_QK_DIM ** -0.5, BF16)
         for u in range(len(units))]
    qk = [lax.dot_general(q[u], k[u], (((1,), (1,)), ((), ())), preferred_element_type=F32)
          for u in range(len(units))]
    qc = [jnp.dot(q[u], c_ref[bi, h].astype(BF16), preferred_element_type=F32)
          for u, (bi, h) in enumerate(units)]

    m_prev = [m_ref[bi, h][:, 0:1] for bi, h in units]
    m_rows = [jnp.maximum(b_col[u] + m_prev[u], row_max[u]) for u in range(len(units))]
    inter = [jnp.exp(b_col[u] + m_prev[u] - m_rows[u]) for u in range(len(units))]
    w = [qk[u] * jnp.exp(dmat[u] - m_rows[u]) for u in range(len(units))]

    for u, (bi, h) in enumerate(units):
        v = proj_ref[slot, rows[u], pv_cols[u]]
        num = inter[u] * qc[u] + jnp.dot(w[u].astype(BF16), v, preferred_element_type=F32)
        qn = jnp.sum(q[u].astype(F32) * n_ref[bi, h], axis=1, keepdims=True)
        den = inter[u] * qn + jnp.sum(w[u], axis=1, keepdims=True)
        hid = num * (1.0 / jnp.maximum(jnp.abs(den), jnp.exp(-m_rows[u])))
        og = proj_ref[slot, rows[u], og_cols[u]].astype(F32)
        o_ref[bi, out_rows, v_cols[u]] = (hid * jax.nn.sigmoid(og)).astype(BF16)

    for u, (bi, h) in enumerate(units):
        v = proj_ref[slot, rows[u], pv_cols[u]]
        m_end = m_rows[u][L - 1:L, :]
        b_end = b_col[u][L - 1:L, :]
        decay = jnp.exp(b_end + m_prev[u] - m_end)
        wk = jnp.exp(b_end - b_col[u] + ig_col[u] - m_end)
        kw = k[u].astype(F32) * wk
        c_ref[bi, h] = decay * c_ref[bi, h] + lax.dot_general(
            kw.astype(BF16), v, (((0,), (0,)), ((), ())), preferred_element_type=F32)
        n_ref[bi, h] = decay * n_ref[bi, h] + jnp.sum(kw, axis=0, keepdims=True)
        m_ref[bi, h] = jnp.broadcast_to(m_end, (1, LANES))


def _mlstm_kernel(xa_ref, xb_ref, w_ref, wg_ref, bg_ref, o_ref,
                  proj_ref, gbuf_ref, c_ref, n_ref, m_ref):
    s = pl.program_id(1)
    last = pl.num_programs(1) - 1
    L = CHUNK

    def project(x_ref, slot):
        _mlstm_project(x_ref, w_ref, wg_ref, bg_ref, proj_ref, gbuf_ref, slot)

    def chunk(slot, out_row0):
        _mlstm_chunk(proj_ref, gbuf_ref, slot, o_ref, out_row0, c_ref, n_ref, m_ref)

    @pl.when(s == 0)
    def _():
        c_ref[...] = jnp.zeros_like(c_ref)
        n_ref[...] = jnp.zeros_like(n_ref)
        m_ref[...] = jnp.zeros_like(m_ref)
        project(xb_ref, 0)

    @pl.when((s > 0) & (s < last))
    def _():
        project(xa_ref, 1)
        chunk(0, 0)
        project(xb_ref, 0)
        chunk(1, L)

    @pl.when(s == last)
    def _():
        project(xa_ref, 1)
        chunk(0, 0)
        chunk(1, L)


def _mlstm_proj_core(x, w_in, w_gates, b_gates, j, batch, seq):
    nc = seq // CHUNK
    v_cols = M_HEADS * M_V_DIM
    assert batch % M_BATCH == 0 and nc % 2 == 0
    const = lambda b, s: (j, 0, 0)
    return pl.pallas_call(
        _mlstm_kernel,
        grid=(batch // M_BATCH, nc // 2 + 1),
        in_specs=[
            pl.BlockSpec((M_BATCH, CHUNK, D_MODEL), lambda b, s: (b, jnp.maximum(2 * s - 1, 0), 0)),
            pl.BlockSpec((M_BATCH, CHUNK, D_MODEL), lambda b, s: (b, jnp.minimum(2 * s, nc - 1), 0)),
            pl.BlockSpec((None, D_MODEL, M_MAIN_COLS), const, pipeline_mode=pl.Buffered(1)),
            pl.BlockSpec((None, D_MODEL, LANES), const),
            pl.BlockSpec((None, 1, LANES), const),
        ],
        out_specs=pl.BlockSpec((M_BATCH, 2 * CHUNK, v_cols), lambda b, s: (b, jnp.maximum(s - 1, 0), 0)),
        out_shape=jax.ShapeDtypeStruct((batch, seq, v_cols), BF16),
        scratch_shapes=[
            pltpu.VMEM((2, M_BATCH * CHUNK, M_MAIN_COLS), BF16),
            pltpu.VMEM((2, M_BATCH * CHUNK, LANES), F32),
            pltpu.VMEM((M_BATCH, M_HEADS, M_QK_DIM, M_V_DIM), F32),
            pltpu.VMEM((M_BATCH, M_HEADS, 1, M_QK_DIM), F32),
            pltpu.VMEM((M_BATCH, M_HEADS, 1, LANES), F32),
        ],
        compiler_params=_compiler_params(("parallel", "arbitrary")),
        name="mlstm_proj_core",
    )(x, x, w_in, w_gates, b_gates)


def _mlstm_weights(mlstm_w_in, mlstm_b_gates, mlstm_w_o):
    pad = LANES - M_GATE_COLS
    w_in = mlstm_w_in.astype(BF16)
    w_gates = jnp.pad(mlstm_w_in[:, :, M_MAIN_COLS:], ((0, 0), (0, 0), (0, pad))).astype(BF16)
    bias = jnp.pad(mlstm_b_gates.astype(F32), ((0, 0), (0, pad)))[:, None, :]
    return w_in, w_gates, bias, mlstm_w_o.astype(BF16)


def _mlstm_mixer(x, w_in, w_gates, bias, w_o, j, g, b, layer, batch, seq):
    hid = _mlstm_proj_core(x.reshape(batch, seq, D_MODEL), w_in, w_gates, bias, j, batch, seq)
    return _outproj_ln(hid.reshape(batch * seq, D_MODEL), w_o, j, x, g, b, layer)


def kernel(x, ffn_w1, ffn_w3, ffn_w2, ln_g, ln_b, att_w_qkv, att_sinks, att_w_o,
           mlstm_w_in, mlstm_b_gates, mlstm_w_o):
    batch, seq, d = x.shape
    assert d == D_MODEL and seq % WINDOW == 0 and (batch * seq) % FFN_TM == 0
    xt = x.reshape(batch * seq, d)
    w1 = ffn_w1.astype(F32)
    w3 = ffn_w3.astype(BF16)
    w2 = ffn_w2.astype(F32)
    g = ln_g.astype(F32).reshape(DEPTH, 3, 1, d)
    b = ln_b.astype(F32).reshape(DEPTH, 3, 1, d)
    wq, wkv, att_wo = _attention_weights(att_w_qkv, att_w_o)
    m_w_in, m_w_gates, m_bias, m_wo = _mlstm_weights(mlstm_w_in, mlstm_b_gates, mlstm_w_o)
    sinks = att_sinks.astype(F32)
    for layer in range(DEPTH):
        xt = _ffn_ln(xt, w1, w3, w2, g, b, layer, 0)
        j = layer // 2
        if layer % 2 == 0:
            xt = _attention_mixer(xt, wq, wkv, att_wo, sinks[j], j, g, b, layer, batch, seq)
        else:
            xt = _mlstm_mixer(xt, m_w_in, m_w_gates, m_bias, m_wo, j, g, b, layer, batch, seq)
        xt = _ffn_ln(xt, w1, w3, w2, g, b, layer, 1)
    return xt.reshape(batch, seq, d).astype(x.dtype)
```

```python
import functools

import jax
import jax.numpy as jnp
from jax import lax
from jax.experimental import pallas as pl
from jax.experimental.pallas import tpu as pltpu

D_MODEL = 2048
DEPTH = 4
D_FF = 5632
ATT_HEADS = 32
ATT_KV_HEADS = 4
ATT_GROUP = ATT_HEADS // ATT_KV_HEADS
ATT_HEAD_DIM = D_MODEL // ATT_HEADS
WINDOW = 128
M_HEADS = 4
M_V_DIM = D_MODEL // M_HEADS
M_QK_DIM = M_V_DIM // 2
CHUNK = 128
DEEPNORM_ALPHA = (2.0 * DEPTH) ** 0.25
LN_EPS = 1e-5

LANES = 128
MASKED = -1e30
NO_KEY_DISTANCE = 1e30
VMEM_LIMIT_BYTES = 60 * 1024 * 1024

BF16 = jnp.bfloat16
F32 = jnp.float32


def _compiler_params(semantics):
    return pltpu.CompilerParams(dimension_semantics=semantics,
                                vmem_limit_bytes=VMEM_LIMIT_BYTES)


def _layer_norm(y, g, b):
    mu = jnp.mean(y, axis=-1, keepdims=True)
    yc = y - mu
    var = jnp.mean(yc * yc, axis=-1, keepdims=True)
    return yc * lax.rsqrt(var + LN_EPS) * g + b


LN_ROWS = 16


def _layer_norm_rows(o_ref, first_row, n_rows, g, b, residual=None):
    for c in range(n_rows // LN_ROWS):
        rows = pl.ds(first_row + c * LN_ROWS, LN_ROWS)
        y = o_ref[rows, :]
        if residual is not None:
            y = y + residual(rows)
        o_ref[rows, :] = _layer_norm(y, g, b)


FFN_TM = 1024
FFN_SUB = 512
FFN_TF = 512
FFN_NF = D_FF // FFN_TF
FFN_PREFETCH_F = FFN_NF // 2
assert 0 < FFN_PREFETCH_F < FFN_NF


def _ffn_kernel(x_hbm, w1_ref, w3_ref, w2_ref, g_ref, b_ref, o_ref, xf_ref, xb_ref, sem):
    i = pl.program_id(0)
    f = pl.program_id(1)

    def x_copy(block):
        return pltpu.make_async_copy(x_hbm.at[pl.ds(block * FFN_TM, FFN_TM), :], xf_ref, sem)

    last = pl.num_programs(1) - 1

    def step(is_first, is_last):
        w1 = w1_ref[...].astype(BF16)
        w2 = w2_ref[...].astype(BF16)
        for r in range(FFN_TM // FFN_SUB):
            rows = pl.ds(r * FFN_SUB, FFN_SUB)
            if is_first:
                xf = xf_ref[rows, :]
                xb = xf.astype(BF16)
                xb_ref[rows, :] = xb
            else:
                xb = xb_ref[rows, :]
            h1 = jnp.dot(xb, w1, preferred_element_type=F32)
            h3 = jnp.dot(xb, w3_ref[...], preferred_element_type=F32)
            act = (0.5 * (h1 * jax.nn.sigmoid(h1) * h3)).astype(BF16)
            part = jnp.dot(act, w2, preferred_element_type=F32)
            if is_last and r > 0:
                _layer_norm_rows(o_ref, (r - 1) * FFN_SUB, FFN_SUB, g_ref[...], b_ref[...])
            o_ref[rows, :] = (DEEPNORM_ALPHA * xf if is_first else o_ref[rows, :]) + part
        if is_last:
            _layer_norm_rows(o_ref, FFN_TM - FFN_SUB, FFN_SUB, g_ref[...], b_ref[...])

    @pl.when(f == 0)
    def _():
        @pl.when(i == 0)
        def _():
            x_copy(0).start()

        x_copy(i).wait()
        step(True, False)

    @pl.when(f == FFN_PREFETCH_F)
    def _():
        @pl.when(i + 1 < pl.num_programs(0))
        def _():
            x_copy(i + 1).start()

    @pl.when((f > 0) & (f < last))
    def _():
        step(False, False)

    @pl.when(f == last)
    def _():
        step(False, True)


def _ln_spec(layer, which):
    return pl.BlockSpec((None, None, 1, D_MODEL), lambda *_: (layer, which, 0, 0))


def _ffn_ln(x, w1, w3, w2, g, b, layer, half):
    t = x.shape[0]
    grid = (t // FFN_TM, FFN_NF)
    return pl.pallas_call(
        _ffn_kernel,
        grid=grid,
        in_specs=[
            pl.BlockSpec(memory_space=pl.ANY),
            pl.BlockSpec((None, None, D_MODEL, FFN_TF), lambda i, f: (layer, half, 0, f)),
            pl.BlockSpec((None, None, D_MODEL, FFN_TF), lambda i, f: (layer, half, 0, f)),
            pl.BlockSpec((None, None, FFN_TF, D_MODEL), lambda i, f: (layer, half, f, 0)),
            _ln_spec(layer, 2 * half),
            _ln_spec(layer, 2 * half),
        ],
        out_specs=pl.BlockSpec((FFN_TM, D_MODEL), lambda i, f: (i, 0)),
        out_shape=jax.ShapeDtypeStruct((t, D_MODEL), F32),
        scratch_shapes=[pltpu.VMEM((FFN_TM, D_MODEL), F32),
                        pltpu.VMEM((FFN_TM, D_MODEL), BF16),
                        pltpu.SemaphoreType.DMA(())],
        compiler_params=_compiler_params(("arbitrary", "arbitrary")),
        name="ffn_ln",
    )(x, w1, w3, w2, g, b)


OUT_TM = 512
OUT_SUB = 256


def _outproj_kernel(a_ref, w_ref, x_ref, g_ref, b_ref, o_ref):
    for r in range(OUT_TM // OUT_SUB):
        rows = pl.ds(r * OUT_SUB, OUT_SUB)
        o_ref[rows, :] = jnp.dot(a_ref[rows, :], w_ref[...], preferred_element_type=F32)
        _layer_norm_rows(o_ref, r * OUT_SUB, OUT_SUB, g_ref[...], b_ref[...],
                         residual=lambda ln_rows: DEEPNORM_ALPHA * x_ref[ln_rows, :])


def _outproj_ln(a, w, j, x, g, b, layer):
    t = x.shape[0]
    return pl.pallas_call(
        _outproj_kernel,
        grid=(t // OUT_TM,),
        in_specs=[
            pl.BlockSpec((OUT_TM, D_MODEL), lambda i: (i, 0)),
            pl.BlockSpec((None, D_MODEL, D_MODEL), lambda i: (j, 0, 0)),
            pl.BlockSpec((OUT_TM, D_MODEL), lambda i: (i, 0)),
            _ln_spec(layer, 1),
            _ln_spec(layer, 1),
        ],
        out_specs=pl.BlockSpec((OUT_TM, D_MODEL), lambda i: (i, 0)),
        out_shape=jax.ShapeDtypeStruct((t, D_MODEL), F32),
        compiler_params=_compiler_params(("parallel",)),
        name="outproj_ln",
    )(a, w, x, g, b)


ATT_KV_COLS = ATT_KV_HEADS * ATT_HEAD_DIM
ATT_SUB = 2 * WINDOW
ATT_K_COL0 = D_MODEL
ATT_V_COL0 = D_MODEL + ATT_KV_COLS
ATT_SLOT_COLS = D_MODEL + 2 * ATT_KV_COLS


def _alibi_slope(head):
    return 2.0 ** (-8.0 * (head + 1) / ATT_HEADS)


def _attn_project(x_ref, wq_ref, wkv_ref, qkv_ref, slot):
    xb = x_ref[...].astype(BF16)
    tile = 2 * ATT_KV_COLS
    for nt in range(D_MODEL // tile):
        cols = slice(nt * tile, (nt + 1) * tile)
        qkv_ref[slot, :, cols] = jnp.dot(xb, wq_ref[:, cols], preferred_element_type=F32).astype(BF16)
        yield
    qkv_ref[slot, :, D_MODEL:] = jnp.dot(xb, wkv_ref[...], preferred_element_type=F32).astype(BF16)


def _attend_block(sink_ref, q_at, kp_at, kc_at, vp_at, vc_at, no_prev, o_ref, out_rows):
    W = WINDOW
    rows = 2 * W
    qi = lax.broadcasted_iota(jnp.int32, (rows, W), 0) & (W - 1)
    j = lax.broadcasted_iota(jnp.int32, (rows, W), 1)
    from_prev = j > qi
    dist = jnp.where(from_prev, W + qi - j, qi - j).astype(F32)
    dist = jnp.where(from_prev & no_prev, NO_KEY_DISTANCE, dist)
    is_lo_row = lax.broadcasted_iota(jnp.int32, (rows, 1), 0) < W
    lo_lane = lax.broadcasted_iota(jnp.int32, (1, LANES), 1) < ATT_HEAD_DIM
    zero = jnp.zeros((), BF16)

    for c2 in range(ATT_KV_HEADS // 2):
        kk = jnp.concatenate([kp_at(c2), kc_at(c2)], axis=0)
        vv = jnp.concatenate([vp_at(c2), vc_at(c2)], axis=0)
        vcat = jnp.concatenate([jnp.where(lo_lane, vv, zero),
                                jnp.where(lo_lane, zero, vv)], axis=0)
        G = range(ATT_GROUP)
        pair = [c2 * ATT_GROUP + g for g in G]
        head_lo = [(2 * c2) * ATT_GROUP + g for g in G]
        head_hi = [(2 * c2 + 1) * ATT_GROUP + g for g in G]
        slope = [jnp.where(is_lo_row, _alibi_slope(head_lo[g]), _alibi_slope(head_hi[g])) for g in G]
        sink = [jnp.where(is_lo_row, sink_ref[head_lo[g]], sink_ref[head_hi[g]]) for g in G]
        qp = [q_at(pair[g]) * jnp.asarray(ATT_HEAD_DIM ** -0.5, BF16) for g in G]
        qs = [jnp.concatenate([jnp.where(lo_lane, qp[g], zero),
                               jnp.where(lo_lane, zero, qp[g])], axis=0) for g in G]
        s2_all = lax.dot_general(jnp.concatenate(qs, axis=0), kk, (((1,), (1,)), ((), ())),
                                 preferred_element_type=F32)
        s2 = [s2_all[g * rows:(g + 1) * rows] for g in G]
        s = [jnp.where(from_prev, s2[g][:, :W], s2[g][:, W:]) - slope[g] * dist for g in G]
        m = [jnp.maximum(jnp.max(s[g], axis=-1, keepdims=True), sink[g]) for g in G]
        p = [jnp.exp(s[g] - m[g]) for g in G]
        denom = [jnp.sum(p[g], axis=-1, keepdims=True) + jnp.exp(sink[g] - m[g]) for g in G]
        p = [p[g] * (1.0 / denom[g]) for g in G]
        pcat = []
        for g in G:
            p_prev = jnp.where(from_prev, p[g], 0.0).astype(BF16)
            p_cur = jnp.where(from_prev, 0.0, p[g]).astype(BF16)
            pcat.append(jnp.concatenate([p_prev[:W], p_cur[:W], p_prev[W:], p_cur[W:]], axis=1))
        o_all = jnp.dot(jnp.concatenate(pcat, axis=0), vcat, preferred_element_type=F32)
        for g in G:
            o_ref[out_rows, LANES * pair[g]:LANES * (pair[g] + 1)] = o_all[g * W:(g + 1) * W].astype(BF16)
        yield


def _attn_kernel(sink_ref, xa_ref, xb_ref, wq_ref, wkv_ref, o_ref, qkv_ref, carry_ref):
    s = pl.program_id(1)
    last = pl.num_programs(1) - 1
    W = WINDOW

    def project(x_ref, slot):
        return _attn_project(x_ref, wq_ref, wkv_ref, qkv_ref, slot)

    def attend(slot, out_row0, no_prev):
        lanes = lambda c2: slice(LANES * c2, LANES * (c2 + 1))
        first, second = slice(0, W), slice(W, 2 * W)
        slot_at = lambda rows, col0: (
            lambda c2: qkv_ref[slot, rows, col0 + LANES * c2:col0 + LANES * (c2 + 1)])
        yield from _attend_block(sink_ref,
                      lambda pair: qkv_ref[slot, first, lanes(pair)],
                      lambda c2: carry_ref[:, lanes(c2)],
                      slot_at(first, ATT_K_COL0),
                      lambda c2: carry_ref[:, ATT_KV_COLS + LANES * c2:ATT_KV_COLS + LANES * (c2 + 1)],
                      slot_at(first, ATT_V_COL0),
                      no_prev, o_ref, slice(out_row0, out_row0 + W))
        yield from _attend_block(sink_ref,
                      lambda pair: qkv_ref[slot, second, lanes(pair)],
                      slot_at(first, ATT_K_COL0), slot_at(second, ATT_K_COL0),
                      slot_at(first, ATT_V_COL0), slot_at(second, ATT_V_COL0),
                      False, o_ref, slice(out_row0 + W, out_row0 + 2 * W))
        carry_ref[...] = qkv_ref[slot, second, ATT_K_COL0:]

    @pl.when(s == 0)
    def _():
        carry_ref[...] = jnp.zeros_like(carry_ref)
        _trace_interleaved(project(xb_ref, 0))

    @pl.when((s > 0) & (s < last))
    def _():
        _trace_interleaved(project(xa_ref, 1), attend(0, 0, s == 1))
        _trace_interleaved(project(xb_ref, 0), attend(1, ATT_SUB, False))

    @pl.when(s == last)
    def _():
        _trace_interleaved(project(xa_ref, 1), attend(0, 0, s == 1))
        _trace_interleaved(attend(1, ATT_SUB, False))


def _attn_proj_core(sinks, x, wq, wkv, j, batch, seq):
    nu = seq // ATT_SUB
    assert nu % 2 == 0
    const = lambda b, s: (j, 0, 0)
    return pl.pallas_call(
        _attn_kernel,
        grid=(batch, nu // 2 + 1),
        in_specs=[
            pl.BlockSpec(memory_space=pltpu.SMEM),
            pl.BlockSpec((None, ATT_SUB, D_MODEL), lambda b, s: (b, jnp.maximum(2 * s - 1, 0), 0)),
            pl.BlockSpec((None, ATT_SUB, D_MODEL), lambda b, s: (b, jnp.minimum(2 * s, nu - 1), 0)),
            pl.BlockSpec((None, D_MODEL, D_MODEL), const, pipeline_mode=pl.Buffered(1)),
            pl.BlockSpec((None, D_MODEL, 2 * ATT_KV_COLS), const, pipeline_mode=pl.Buffered(1)),
        ],
        out_specs=pl.BlockSpec((None, 2 * ATT_SUB, D_MODEL), lambda b, s: (b, jnp.maximum(s - 1, 0), 0)),
        out_shape=jax.ShapeDtypeStruct((batch, seq, D_MODEL), BF16),
        scratch_shapes=[
            pltpu.VMEM((2, ATT_SUB, ATT_SLOT_COLS), BF16),
            pltpu.VMEM((WINDOW, 2 * ATT_KV_COLS), BF16),
        ],
        compiler_params=_compiler_params(("parallel", "arbitrary")),
        name="swa_proj_core",
    )(sinks, x, x, wq, wkv)


def _to_head_pair_order(w, axis):
    shape = w.shape
    split = shape[:axis] + (ATT_KV_HEADS // 2, 2, ATT_GROUP, ATT_HEAD_DIM) + shape[axis + 1:]
    order = list(range(len(split)))
    order[axis + 1], order[axis + 2] = order[axis + 2], order[axis + 1]
    return w.reshape(split).transpose(order).reshape(shape)


def _attention_weights(att_w_qkv, att_w_o):
    nq = ATT_HEADS * ATT_HEAD_DIM
    wq = _to_head_pair_order(att_w_qkv[:, :, :nq], 2).astype(BF16)
    wkv = att_w_qkv[:, :, nq:].astype(BF16)
    wo = _to_head_pair_order(att_w_o, 1).astype(BF16)
    return wq, wkv, wo


def _attention_mixer(x, wq, wkv, wo, sinks, j, g, b, layer, batch, seq):
    o = _attn_proj_core(sinks, x.reshape(batch, seq, D_MODEL), wq, wkv, j, batch, seq)
    return _outproj_ln(o.reshape(batch * seq, D_MODEL), wo, j, x, g, b, layer)


MPROJ_TN = 1536
M_MAIN_COLS = 2 * M_HEADS * M_QK_DIM + 2 * M_HEADS * M_V_DIM
M_GATE_COLS = 2 * M_HEADS
M_K_COL0 = M_HEADS * M_QK_DIM
M_V_COL0 = 2 * M_HEADS * M_QK_DIM
M_OG_COL0 = M_V_COL0 + M_HEADS * M_V_DIM
M_BATCH = 2


def _log_sigmoid(x):
    return jnp.minimum(x, 0.0) - jnp.log1p(jnp.exp(-jnp.abs(x)))


def _mlstm_project(x_ref, w_ref, wg_ref, bg_ref, proj_ref, gbuf_ref, slot):
    xb = jnp.concatenate([x_ref[bi] for bi in range(M_BATCH)], axis=0).astype(BF16)
    for nt in range(M_MAIN_COLS // MPROJ_TN):
        cols = slice(nt * MPROJ_TN, (nt + 1) * MPROJ_TN)
        proj_ref[slot, :, cols] = jnp.dot(xb, w_ref[:, cols], preferred_element_type=F32).astype(BF16)
        yield
    gbuf_ref[slot] = jnp.dot(xb, wg_ref[...], preferred_element_type=F32) + bg_ref[...]


def _trace_interleaved(*stage_generators):
    live = list(stage_generators)
    while live:
        for gen in list(live):
            if next(gen, StopIteration) is StopIteration:
                live.remove(gen)


def _mlstm_chunk(proj_ref, gbuf_ref, slot, o_ref, out_row0, c_ref, n_ref, m_ref):
    L = CHUNK
    t_idx = lax.broadcasted_iota(jnp.int32, (L, L), 0)
    s_idx = lax.broadcasted_iota(jnp.int32, (L, L), 1)
    causal = s_idx <= t_idx

    units = [(bi, h) for bi in range(M_BATCH) for h in range(M_HEADS)]
    rows = [slice(bi * L, (bi + 1) * L) for bi, _ in units]
    out_rows = slice(out_row0, out_row0 + L)
    qk_cols = [slice(h * M_QK_DIM, (h + 1) * M_QK_DIM) for _, h in units]
    k_cols = [slice(M_K_COL0 + h * M_QK_DIM, M_K_COL0 + (h + 1) * M_QK_DIM) for _, h in units]
    v_cols = [slice(h * M_V_DIM, (h + 1) * M_V_DIM) for _, h in units]
    pv_cols = [slice(M_V_COL0 + h * M_V_DIM, M_V_COL0 + (h + 1) * M_V_DIM) for _, h in units]
    og_cols = [slice(M_OG_COL0 + h * M_V_DIM, M_OG_COL0 + (h + 1) * M_V_DIM) for _, h in units]
    gates = [gbuf_ref[slot, bi * L:(bi + 1) * L, :] for bi in range(M_BATCH)]
    gates_t = [gt.T for gt in gates]

    ig_col, b_col, dmat, row_max = [], [], [], []
    for bi, h in units:
        ig_col.append(gates[bi][:, h:h + 1])
        ig_row = gates_t[bi][h:h + 1, :]
        lf_col = _log_sigmoid(gates[bi][:, M_HEADS + h:M_HEADS + h + 1])
        lf_row = _log_sigmoid(gates_t[bi][M_HEADS + h:M_HEADS + h + 1, :])
        bc = jnp.sum(jnp.where(causal, lf_row, 0.0), axis=1, keepdims=True)
        br = jnp.sum(jnp.where(t_idx <= s_idx, lf_col, 0.0), axis=0, keepdims=True)
        d = jnp.where(causal, bc - br + ig_row, MASKED)
        b_col.append(bc)
        dmat.append(d)
        row_max.append(jnp.max(d, axis=1, keepdims=True))
    yield

    q = [proj_ref[slot, rows[u], qk_cols[u]] for u in range(len(units))]
    k = [proj_ref[slot, rows[u], k_cols[u]] * jnp.asarray(M_QK_DIM ** -0.5, BF16)
         for u in range(len(units))]
    qk = [lax.dot_general(q[u], k[u], (((1,), (1,)), ((), ())), preferred_element_type=F32)
          for u in range(len(units))]
    qc = [jnp.dot(q[u], c_ref[bi, h].astype(BF16), preferred_element_type=F32)
          for u, (bi, h) in enumerate(units)]
    yield

    m_prev = [m_ref[bi, h][:, 0:1] for bi, h in units]
    m_rows = [jnp.maximum(b_col[u] + m_prev[u], row_max[u]) for u in range(len(units))]
    inter = [jnp.exp(b_col[u] + m_prev[u] - m_rows[u]) for u in range(len(units))]
    w = [qk[u] * jnp.exp(dmat[u] - m_rows[u]) for u in range(len(units))]

    for u, (bi, h) in enumerate(units):
        v = proj_ref[slot, rows[u], pv_cols[u]]
        num = inter[u] * qc[u] + jnp.dot(w[u].astype(BF16), v, preferred_element_type=F32)
        qn = jnp.sum(q[u].astype(F32) * n_ref[bi, h], axis=1, keepdims=True)
        den = inter[u] * qn + jnp.sum(w[u], axis=1, keepdims=True)
        hid = num * (1.0 / jnp.maximum(jnp.abs(den), jnp.exp(-m_rows[u])))
        og = proj_ref[slot, rows[u], og_cols[u]].astype(F32)
        o_ref[bi, out_rows, v_cols[u]] = (hid * jax.nn.sigmoid(og)).astype(BF16)
        if u == len(units) // 2 - 1 or u == len(units) - 1:
            yield

    for u, (bi, h) in enumerate(units):
        v = proj_ref[slot, rows[u], pv_cols[u]]
        m_end = m_rows[u][L - 1:L, :]
        b_end = b_col[u][L - 1:L, :]
        decay = jnp.exp(b_end + m_prev[u] - m_end)
        wk = jnp.exp(b_end - b_col[u] + ig_col[u] - m_end)
        kw = k[u].astype(F32) * wk
        c_ref[bi, h] = decay * c_ref[bi, h] + lax.dot_general(
            kw.astype(BF16), v, (((0,), (0,)), ((), ())), preferred_element_type=F32)
        n_ref[bi, h] = decay * n_ref[bi, h] + jnp.sum(kw, axis=0, keepdims=True)
        m_ref[bi, h] = jnp.broadcast_to(m_end, (1, LANES))


def _mlstm_kernel(xa_ref, xb_ref, w_ref, wg_ref, bg_ref, o_ref,
                  proj_ref, gbuf_ref, c_ref, n_ref, m_ref):
    s = pl.program_id(1)
    last = pl.num_programs(1) - 1
    L = CHUNK

    def project(x_ref, slot):
        return _mlstm_project(x_ref, w_ref, wg_ref, bg_ref, proj_ref, gbuf_ref, slot)

    def chunk(slot, out_row0):
        return _mlstm_chunk(proj_ref, gbuf_ref, slot, o_ref, out_row0, c_ref, n_ref, m_ref)

    @pl.when(s == 0)
    def _():
        c_ref[...] = jnp.zeros_like(c_ref)
        n_ref[...] = jnp.zeros_like(n_ref)
        m_ref[...] = jnp.zeros_like(m_ref)
        _trace_interleaved(project(xb_ref, 0))

    @pl.when((s > 0) & (s < last))
    def _():
        _trace_interleaved(project(xa_ref, 1), chunk(0, 0))
        _trace_interleaved(project(xb_ref, 0), chunk(1, L))

    @pl.when(s == last)
    def _():
        _trace_interleaved(project(xa_ref, 1), chunk(0, 0))
        _trace_interleaved(chunk(1, L))


def _mlstm_proj_core(x, w_in, w_gates, b_gates, j, batch, seq):
    nc = seq // CHUNK
    v_cols = M_HEADS * M_V_DIM
    assert batch % M_BATCH == 0 and nc % 2 == 0
    const = lambda b, s: (j, 0, 0)
    return pl.pallas_call(
        _mlstm_kernel,
        grid=(batch // M_BATCH, nc // 2 + 1),
        in_specs=[
            pl.BlockSpec((M_BATCH, CHUNK, D_MODEL), lambda b, s: (b, jnp.maximum(2 * s - 1, 0), 0)),
            pl.BlockSpec((M_BATCH, CHUNK, D_MODEL), lambda b, s: (b, jnp.minimum(2 * s, nc - 1), 0)),
            pl.BlockSpec((None, D_MODEL, M_MAIN_COLS), const, pipeline_mode=pl.Buffered(1)),
            pl.BlockSpec((None, D_MODEL, LANES), const),
            pl.BlockSpec((None, 1, LANES), const),
        ],
        out_specs=pl.BlockSpec((M_BATCH, 2 * CHUNK, v_cols), lambda b, s: (b, jnp.maximum(s - 1, 0), 0)),
        out_shape=jax.ShapeDtypeStruct((batch, seq, v_cols), BF16),
        scratch_shapes=[
            pltpu.VMEM((2, M_BATCH * CHUNK, M_MAIN_COLS), BF16),
            pltpu.VMEM((2, M_BATCH * CHUNK, LANES), F32),
            pltpu.VMEM((M_BATCH, M_HEADS, M_QK_DIM, M_V_DIM), F32),
            pltpu.VMEM((M_BATCH, M_HEADS, 1, M_QK_DIM), F32),
            pltpu.VMEM((M_BATCH, M_HEADS, 1, LANES), F32),
        ],
        compiler_params=_compiler_params(("parallel", "arbitrary")),
        name="mlstm_proj_core",
    )(x, x, w_in, w_gates, b_gates)


def _mlstm_weights(mlstm_w_in, mlstm_b_gates, mlstm_w_o):
    pad = LANES - M_GATE_COLS
    w_in = mlstm_w_in.astype(BF16)
    w_gates = jnp.pad(mlstm_w_in[:, :, M_MAIN_COLS:], ((0, 0), (0, 0), (0, pad))).astype(BF16)
    bias = jnp.pad(mlstm_b_gates.astype(F32), ((0, 0), (0, pad)))[:, None, :]
    return w_in, w_gates, bias, mlstm_w_o.astype(BF16)


def _mlstm_mixer(x, w_in, w_gates, bias, w_o, j, g, b, layer, batch, seq):
    hid = _mlstm_proj_core(x.reshape(batch, seq, D_MODEL), w_in, w_gates, bias, j, batch, seq)
    return _outproj_ln(hid.reshape(batch * seq, D_MODEL), w_o, j, x, g, b, layer)


def kernel(x, ffn_w1, ffn_w3, ffn_w2, ln_g, ln_b, att_w_qkv, att_sinks, att_w_o,
           mlstm_w_in, mlstm_b_gates, mlstm_w_o):
    batch, seq, d = x.shape
    assert d == D_MODEL and seq % WINDOW == 0 and (batch * seq) % FFN_TM == 0
    xt = x.reshape(batch * seq, d)
    w1 = ffn_w1.astype(F32)
    w3 = ffn_w3.astype(BF16)
    w2 = ffn_w2.astype(F32)
    g = ln_g.astype(F32).reshape(DEPTH, 3, 1, d)
    b = ln_b.astype(F32).reshape(DEPTH, 3, 1, d)
    wq, wkv, att_wo = _attention_weights(att_w_qkv, att_w_o)
    m_w_in, m_w_gates, m_bias, m_wo = _mlstm_weights(mlstm_w_in, mlstm_b_gates, mlstm_w_o)
    sinks = att_sinks.astype(F32)
    for layer in range(DEPTH):
        xt = _ffn_ln(xt, w1, w3, w2, g, b, layer, 0)
        j = layer // 2
        if layer % 2 == 0:
            xt = _attention_mixer(xt, wq, wkv, att_wo, sinks[j], j, g, b, layer, batch, seq)
        else:
            xt = _mlstm_mixer(xt, m_w_in, m_w_gates, m_bias, m_wo, j, g, b, layer, batch, seq)
        xt = _ffn_ln(xt, w1, w3, w2, g, b, layer, 1)
    return xt.reshape(batch, seq, d).astype(x.dtype)
```
